```python
import math
import jax, jax.numpy as jnp
from jax import lax
import numpy as np

D_MODEL = 2048
BATCH = 8
SEQ = 2048
DEPTH = 1
DEC_BATCH = 1
DEC_SEQ = 8192
PAST_LEN = 128

NORM_EPS = 1e-6
D_FF = 5632
RG_WIDTH = 1024
RG_BLOCKS = 16
RG_BLOCK_W = RG_WIDTH // RG_BLOCKS
RG_C = 8.0
CONV_W = 4
CONV_PAD_L = 2
CONV_PAD_R = 1
MLA_HEADS = 8
Q_LORA = 512
KV_LORA = 256
QK_NOPE = 128
QK_ROPE = 64
V_DIM = 128
MLA_WIDTH = MLA_HEADS * V_DIM
ROPE_THETA = 10000.0
Q_BLOCK = 128
IN_SPLITS = (RG_WIDTH, 2 * RG_WIDTH, 2 * RG_WIDTH + Q_LORA, 2 * RG_WIDTH + Q_LORA + KV_LORA)
D_IN = 2 * RG_WIDTH + Q_LORA + KV_LORA + QK_ROPE
D_MIX = RG_WIDTH + MLA_WIDTH

kernel_name = "hymba_rglru_mla_macaron_encoder"


def rms_norm(x, g):
    xf = x.astype(jnp.float32)
    y = xf * lax.rsqrt(jnp.mean(xf * xf, axis=-1, keepdims=True) + NORM_EPS)
    return (y * g.astype(jnp.float32)).astype(x.dtype)


def swiglu(x, w_gate, w_up, w_down):
    return (jax.nn.silu(x @ w_gate) * (x @ w_up)) @ w_down


def centred_depthwise_conv(x, w, b):
    c = x.shape[-1]
    y = lax.conv_general_dilated(
        x, w[:, None, :].astype(x.dtype), window_strides=(1,),
        padding=[(CONV_PAD_L, CONV_PAD_R)],
        dimension_numbers=("NWC", "WIO", "NWC"), feature_group_count=c)
    return y + b


def _linear_rec_combine(left, right):
    a1, b1 = left
    a2, b2 = right
    return a1 * a2, a2 * b1 + b2


def rg_lru_direction(xc, w_a, b_a, w_x, b_x, lam, reverse):
    bsz, s, c = xc.shape
    xb = xc.reshape(bsz, s, RG_BLOCKS, RG_BLOCK_W)
    r = jax.nn.sigmoid(jnp.einsum("bshi,hij->bshj", xb, w_a) + b_a).reshape(bsz, s, c)
    i = jax.nn.sigmoid(jnp.einsum("bshi,hij->bshj", xb, w_x) + b_x).reshape(bsz, s, c)
    log_a = -RG_C * r.astype(jnp.float32) * jax.nn.softplus(-lam.astype(jnp.float32))
    a = jnp.exp(log_a)
    mult = jnp.sqrt(-jnp.expm1(2.0 * log_a))
    bterm = mult * (i * xc).astype(jnp.float32)
    _, h = lax.associative_scan(_linear_rec_combine, (a, bterm), axis=1, reverse=reverse)
    return h


def rope_tables(seq):
    inv = 1.0 / (ROPE_THETA ** (jnp.arange(0, QK_ROPE, 2, dtype=jnp.float32) / QK_ROPE))
    ang = jnp.arange(seq, dtype=jnp.float32)[:, None] * inv[None, :]
    return jnp.cos(ang), jnp.sin(ang)


def apply_rope(x, cos, sin):
    xf = x.astype(jnp.float32)
    x1, x2 = jnp.split(xf, 2, axis=-1)
    out = jnp.concatenate([x1 * cos - x2 * sin, x2 * cos + x1 * sin], axis=-1)
    return out.astype(x.dtype)


def mla_attention(qn, qr, kn, kr, v):
    bsz, s, h, _ = qn.shape
    nb = s // Q_BLOCK
    scale = 1.0 / math.sqrt(QK_NOPE + QK_ROPE)

    def block(args):
        qn_b, qr_b = args
        sc = (jnp.einsum("bqhd,bkhd->bhqk", qn_b, kn, preferred_element_type=jnp.float32)
              + jnp.einsum("bqhr,bkr->bhqk", qr_b, kr, preferred_element_type=jnp.float32))
        p = jax.nn.softmax(sc * scale, axis=-1).astype(v.dtype)
        return jnp.einsum("bhqk,bkhd->bqhd", p, v)

    qn_blk = qn.reshape(bsz, nb, Q_BLOCK, h, QK_NOPE).transpose(1, 0, 2, 3, 4)
    qr_blk = qr.reshape(bsz, nb, Q_BLOCK, h, QK_ROPE).transpose(1, 0, 2, 3, 4)
    out = lax.map(block, (qn_blk, qr_blk))
    return out.transpose(1, 0, 2, 3, 4).reshape(bsz, s, h * V_DIM)


def hybrid_mixer(h, w_in, conv_w, conv_b, rg_w_a, rg_b_a, rg_w_x, rg_b_x, rg_lambda,
                 q_a_norm, w_q_b, kv_a_norm, w_kv_b, rg_out_norm, attn_out_norm, w_out):
    bsz, s, _ = h.shape
    z = h @ w_in
    x_rg, g_rg, q_c, kv_c, k_r = jnp.split(z, IN_SPLITS, axis=-1)
    xc = centred_depthwise_conv(x_rg, conv_w, conv_b)
    h_f = rg_lru_direction(xc, rg_w_a[0], rg_b_a[0], rg_w_x[0], rg_b_x[0], rg_lambda[0], False)
    h_b = rg_lru_direction(xc, rg_w_a[1], rg_b_a[1], rg_w_x[1], rg_b_x[1], rg_lambda[1], True)
    y_rg = ((h_f + h_b) * jax.nn.gelu(g_rg.astype(jnp.float32))).astype(h.dtype)
    q = (rms_norm(q_c, q_a_norm) @ w_q_b).reshape(bsz, s, MLA_HEADS, QK_NOPE + QK_ROPE)
    qn, qr = q[..., :QK_NOPE], q[..., QK_NOPE:]
    kv = (rms_norm(kv_c, kv_a_norm) @ w_kv_b).reshape(bsz, s, MLA_HEADS, QK_NOPE + V_DIM)
    kn, v = kv[..., :QK_NOPE], kv[..., QK_NOPE:]
    cos, sin = rope_tables(s)
    qr = apply_rope(qr, cos[:, None, :], sin[:, None, :])
    kr = apply_rope(k_r, cos, sin)
    y_att = mla_attention(qn, qr, kn, kr, v)
    y = jnp.concatenate([rms_norm(y_rg, rg_out_norm), rms_norm(y_att, attn_out_norm)], axis=-1)
    return y @ w_out


def setup_inputs(seed: int = 0) -> dict:
    key = jax.random.key(seed)
    ks = jax.random.split(key, 32)

    def nrm(k, shape, fan_in):
        return jax.random.normal(k, shape, jnp.float32) * (fan_in ** -0.5)

    def gain(k, shape):
        return 1.0 + 0.01 * jax.random.normal(k, shape, jnp.float32)

    L = DEPTH
    u = jax.random.uniform(ks[13], (L, 2, RG_WIDTH), jnp.float32, minval=0.9, maxval=0.999)
    sig = u ** (1.0 / RG_C)
    rg_lambda = jnp.log(sig / (1.0 - sig))
    return {
        "x_prompt": jax.random.normal(ks[0], (BATCH, SEQ, D_MODEL), jnp.float32),
        "x_sample": jax.random.normal(ks[1], (DEC_BATCH, DEC_SEQ, D_MODEL), jnp.float32),
        "ffn1_norm": gain(ks[2], (L, D_MODEL)),
        "ffn1_w_gate": nrm(ks[3], (L, D_MODEL, D_FF), D_MODEL),
        "ffn1_w_up": nrm(ks[4], (L, D_MODEL, D_FF), D_MODEL),
        "ffn1_w_down": nrm(ks[5], (L, D_FF, D_MODEL), D_FF),
        "mix_norm": gain(ks[6], (L, D_MODEL)),
        "w_in": nrm(ks[7], (L, D_MODEL, D_IN), D_MODEL),
        "conv_w": nrm(ks[8], (L, CONV_W, RG_WIDTH), CONV_W),
        "conv_b": 0.01 * jax.random.normal(ks[9], (L, RG_WIDTH), jnp.float32),
        "rg_w_a": nrm(ks[10], (L, 2, RG_BLOCKS, RG_BLOCK_W, RG_BLOCK_W), RG_BLOCK_W),
        "rg_b_a": 0.01 * jax.random.normal(ks[11], (L, 2, RG_BLOCKS, RG_BLOCK_W), jnp.float32),
        "rg_w_x": nrm(ks[12], (L, 2, RG_BLOCKS, RG_BLOCK_W, RG_BLOCK_W), RG_BLOCK_W),
        "rg_b_x": 0.01 * jax.random.normal(ks[14], (L, 2, RG_BLOCKS, RG_BLOCK_W), jnp.float32),
        "rg_lambda": rg_lambda,
        "q_a_norm": gain(ks[15], (L, Q_LORA)),
        "w_q_b": nrm(ks[16], (L, Q_LORA, MLA_HEADS * (QK_NOPE + QK_ROPE)), Q_LORA),
        "kv_a_norm": gain(ks[17], (L, KV_LORA)),
        "w_kv_b": nrm(ks[18], (L, KV_LORA, MLA_HEADS * (QK_NOPE + V_DIM)), KV_LORA),
        "rg_out_norm": gain(ks[19], (L, RG_WIDTH)),
        "attn_out_norm": gain(ks[20], (L, MLA_WIDTH)),
        "w_out": nrm(ks[21], (L, D_MIX, D_MODEL), D_MIX),
        "ffn2_norm": gain(ks[22], (L, D_MODEL)),
        "ffn2_w_gate": nrm(ks[23], (L, D_MODEL, D_FF), D_MODEL),
        "ffn2_w_up": nrm(ks[24], (L, D_MODEL, D_FF), D_MODEL),
        "ffn2_w_down": nrm(ks[25], (L, D_FF, D_MODEL), D_FF),
        "final_norm": gain(ks[26], (D_MODEL,)),
    }


def reference(x_prompt, x_sample, ffn1_norm, ffn1_w_gate, ffn1_w_up, ffn1_w_down, mix_norm,
              w_in, conv_w, conv_b, rg_w_a, rg_b_a, rg_w_x, rg_b_x, rg_lambda,
              q_a_norm, w_q_b, kv_a_norm, w_kv_b, rg_out_norm, attn_out_norm, w_out,
              ffn2_norm, ffn2_w_gate, ffn2_w_up, ffn2_w_down, final_norm):
    def trunk(x):
        for l in range(DEPTH):
            x = x + 0.5 * swiglu(rms_norm(x, ffn1_norm[l]), ffn1_w_gate[l], ffn1_w_up[l], ffn1_w_down[l])
            x = x + hybrid_mixer(rms_norm(x, mix_norm[l]), w_in[l], conv_w[l], conv_b[l],
                                 rg_w_a[l], rg_b_a[l], rg_w_x[l], rg_b_x[l], rg_lambda[l],
                                 q_a_norm[l], w_q_b[l], kv_a_norm[l], w_kv_b[l],
                                 rg_out_norm[l], attn_out_norm[l], w_out[l])
            x = x + 0.5 * swiglu(rms_norm(x, ffn2_norm[l]), ffn2_w_gate[l], ffn2_w_up[l], ffn2_w_down[l])
        return rms_norm(x, final_norm)

    y_prompt = trunk(x_prompt)
    y_sample = trunk(x_sample)
    return (y_prompt, y_sample)
```

```python
import functools
import math

import jax
import jax.numpy as jnp
from jax import lax
from jax.experimental import pallas as pl
from jax.experimental.pallas import tpu as pltpu

F32 = jnp.float32
BF16 = jnp.bfloat16

NORM_EPS = 1e-6
RG_WIDTH = 1024
RG_BLOCKS = 16
RG_BLOCK_W = RG_WIDTH // RG_BLOCKS
RG_C = 8.0
CONV_W = 4
CONV_PAD_L = 2
MLA_HEADS = 8
Q_LORA = 512
KV_LORA = 256
QK_NOPE = 128
QK_ROPE = 64
V_DIM = 128
ROPE_THETA = 10000.0
ROPE_HALF = QK_ROPE // 2
HEAD_PAD = 256

V7X_SUBLANES = 8
V7X_LANES = 128
V7X_VMEM_LIMIT_BYTES = 60000 * 1024


def _pick(dim, pref):
    t = min(dim, pref)
    while dim % t:
        t //= 2
    return t


def _rms(x, g):
    return x * lax.rsqrt(jnp.mean(x * x, axis=-1, keepdims=True) + NORM_EPS) * g


def _const_spec(shape):
    nd = len(shape)
    return pl.BlockSpec(shape, lambda *_: (0,) * nd, pipeline_mode=pl.Buffered(1))


def _ffn_body(x_ref, g_ref, wg_ref, wu_ref, wd_ref, pg_ref, o_ref, xn_ref, *, final_norm):
    f = pl.program_id(1)

    @pl.when(f == 0)
    def _():
        x = x_ref[...]
        xn_ref[...] = _rms(x, g_ref[...]).astype(BF16)
        o_ref[...] = x

    xn = xn_ref[...]
    gate = jnp.dot(xn, wg_ref[...], preferred_element_type=F32)
    up = jnp.dot(xn, wu_ref[...], preferred_element_type=F32)
    h = (gate * jax.nn.sigmoid(gate)) * (0.5 * up)
    o_ref[...] += jnp.dot(h.astype(BF16), wd_ref[...], preferred_element_type=F32)

    if final_norm:
        @pl.when(f == pl.num_programs(1) - 1)
        def _():
            o_ref[...] = _rms(o_ref[...], pg_ref[...])


def _ffn(x, g, wg, wu, wd, pg, *, final_norm):
    t, d = x.shape
    dff = wg.shape[1]
    tm = _pick(t, 512)
    tf = _pick(dff, 512)
    return pl.pallas_call(
        functools.partial(_ffn_body, final_norm=final_norm),
        grid=(t // tm, dff // tf),
        in_specs=[
            pl.BlockSpec((tm, d), lambda i, f: (i, 0)),
            _const_spec((1, d)),
            pl.BlockSpec((d, tf), lambda i, f: (0, f)),
            pl.BlockSpec((d, tf), lambda i, f: (0, f)),
            pl.BlockSpec((tf, d), lambda i, f: (f, 0)),
            _const_spec((1, d)),
        ],
        out_specs=pl.BlockSpec((tm, d), lambda i, f: (i, 0)),
        out_shape=jax.ShapeDtypeStruct((t, d), F32),
        scratch_shapes=[pltpu.VMEM((tm, d), BF16)],
        compiler_params=pltpu.CompilerParams(
            dimension_semantics=("parallel", "arbitrary"),
            vmem_limit_bytes=V7X_VMEM_LIMIT_BYTES),
        name="ffn",
    )(x, g, wg, wu, wd, pg)


def _mixer_in_body(x_ref, g_ref, win_ref, qg_ref, wq_ref, kvg_ref, wkv_ref, tq_ref, tkc_ref, tks_ref,
                   xrg_ref, grg_ref, q_ref, k_ref, v_ref):
    hn = _rms(x_ref[0], g_ref[...]).astype(BF16)
    z = jnp.dot(hn, win_ref[...], preferred_element_type=F32)
    o = 0
    xrg_ref[0] = z[:, o:o + RG_WIDTH]
    o += RG_WIDTH
    grg_ref[0] = z[:, o:o + RG_WIDTH]
    o += RG_WIDTH
    qc = z[:, o:o + Q_LORA]
    o += Q_LORA
    kvc = z[:, o:o + KV_LORA]
    o += KV_LORA
    ka = z[:, o:o + V7X_LANES]
    kb = z[:, o + V7X_LANES:o + 2 * V7X_LANES]

    q = jnp.dot(_rms(qc, qg_ref[...]).astype(BF16), wq_ref[...], preferred_element_type=F32)
    kv = jnp.dot(_rms(kvc, kvg_ref[...]).astype(BF16), wkv_ref[...], preferred_element_type=F32)
    kr = (ka * tkc_ref[...] + kb * tks_ref[...]).astype(BF16)
    tq = tq_ref[...]
    nv = MLA_HEADS * QK_NOPE
    for h in range(MLA_HEADS):
        q_ref[0, h] = (q[:, h * HEAD_PAD:(h + 1) * HEAD_PAD] * tq).astype(BF16)
        k_ref[0, h, :, 0:QK_NOPE] = kv[:, h * QK_NOPE:(h + 1) * QK_NOPE].astype(BF16)
        k_ref[0, h, :, QK_NOPE:HEAD_PAD] = kr
        v_ref[0, h] = kv[:, nv + h * V_DIM:nv + (h + 1) * V_DIM].astype(BF16)


def _mixer_in(x, g, win, qg, wq, kvg, wkv, tq, tkc, tks):
    b, s, d = x.shape
    tm = _pick(s, 512)
    din = win.shape[1]
    nh = MLA_HEADS
    return pl.pallas_call(
        _mixer_in_body,
        grid=(b, s // tm),
        in_specs=[
            pl.BlockSpec((1, tm, d), lambda bi, i: (bi, i, 0)),
            _const_spec((1, d)),
            _const_spec((d, din)),
            _const_spec((1, Q_LORA)),
            _const_spec((Q_LORA, nh * HEAD_PAD)),
            _const_spec((1, KV_LORA)),
            _const_spec((KV_LORA, nh * (QK_NOPE + V_DIM))),
            pl.BlockSpec((tm, HEAD_PAD), lambda bi, i: (i, 0)),
            pl.BlockSpec((tm, V7X_LANES), lambda bi, i: (i, 0)),
            pl.BlockSpec((tm, V7X_LANES), lambda bi, i: (i, 0)),
        ],
        out_specs=[
            pl.BlockSpec((1, tm, RG_WIDTH), lambda bi, i: (bi, i, 0)),
            pl.BlockSpec((1, tm, RG_WIDTH), lambda bi, i: (bi, i, 0)),
            pl.BlockSpec((1, nh, tm, HEAD_PAD), lambda bi, i: (bi, 0, i, 0)),
            pl.BlockSpec((1, nh, tm, HEAD_PAD), lambda bi, i: (bi, 0, i, 0)),
            pl.BlockSpec((1, nh, tm, V_DIM), lambda bi, i: (bi, 0, i, 0)),
        ],
        out_shape=[
            jax.ShapeDtypeStruct((b, s, RG_WIDTH), F32),
            jax.ShapeDtypeStruct((b, s, RG_WIDTH), F32),
            jax.ShapeDtypeStruct((b, nh, s, HEAD_PAD), BF16),
            jax.ShapeDtypeStruct((b, nh, s, HEAD_PAD), BF16),
            jax.ShapeDtypeStruct((b, nh, s, V_DIM), BF16),
        ],
        compiler_params=pltpu.CompilerParams(
            dimension_semantics=("parallel", "parallel"),
            vmem_limit_bytes=V7X_VMEM_LIMIT_BYTES),
        name="mixer_in",
    )(x, g, win, qg, wq, kvg, wkv, tq, tkc, tks)


def _scan_block(a, b, rows, reverse):
    n = V7X_SUBLANES
    for s in (1, 2, 4):
        if reverse:
            a_s = pltpu.roll(a, n - s, 0)
            b_s = pltpu.roll(b, n - s, 0)
            valid = rows < n - s
        else:
            a_s = pltpu.roll(a, s, 0)
            b_s = pltpu.roll(b, s, 0)
            valid = rows >= s
        b = jnp.where(valid, a * b_s + b, b)
        a = jnp.where(valid, a * a_s, a)
    return a, b


def _rglru_body(x_ref, g_ref, cw_ref, cb_ref, w_ref, bias_ref, lam_ref, y_ref, xpad_ref, hf_ref, *, tc):
    s, ct = x_ref.shape[1], x_ref.shape[2]
    n = V7X_SUBLANES
    nchunks = s // tc
    halo = n

    zeros = jnp.zeros((halo, ct), F32)
    xpad_ref[0:halo, :] = zeros
    xpad_ref[halo + s:halo + s + halo, :] = zeros
    xpad_ref[halo:halo + s, :] = x_ref[0]

    cw = cw_ref[...]
    cb = cb_ref[...]
    sp = jax.nn.softplus(-lam_ref[...])
    rows = lax.broadcasted_iota(jnp.int32, (n, ct), 0)

    def direction(c, carry, reverse):
        t0 = pl.multiple_of(c * tc, tc)
        blk = xpad_ref[pl.ds(t0, tc + 2 * halo), :]
        xc = cb
        for k in range(CONV_W):
            o = halo - CONV_PAD_L + k
            xc = xc + cw[k:k + 1, :] * blk[o:o + tc, :]
        d = 1 if reverse else 0
        gates = jnp.dot(xc.astype(BF16), w_ref[0, :, 2 * d * ct:2 * (d + 1) * ct],
                        preferred_element_type=F32) + bias_ref[0, :, 2 * d * ct:2 * (d + 1) * ct]
        r = jax.nn.sigmoid(gates[:, :ct])
        i = jax.nn.sigmoid(gates[:, ct:])
        log_a = (-RG_C) * r * sp[d:d + 1, :]
        a = jnp.exp(log_a)
        bterm = jnp.sqrt(1.0 - jnp.exp(2.0 * log_a)) * (i * xc)
        nb = tc // n
        order = range(nb - 1, -1, -1) if reverse else range(nb)
        for j in order:
            a_c, h = _scan_block(a[j * n:(j + 1) * n, :], bterm[j * n:(j + 1) * n, :], rows, reverse)
            h = h + a_c * carry
            carry = h[0:1, :] if reverse else h[n - 1:n, :]
            tj = pl.multiple_of(t0 + j * n, n)
            if reverse:
                hsum = hf_ref[pl.ds(tj, n), :] + h
                y_ref[0, pl.ds(tj, n), :] = hsum * jax.nn.gelu(g_ref[0, pl.ds(tj, n), :])
            else:
                hf_ref[pl.ds(tj, n), :] = h
        return carry

    h0 = jnp.zeros((1, ct), F32)
    lax.fori_loop(0, nchunks, lambda c, carry: direction(c, carry, False), h0)
    lax.fori_loop(0, nchunks, lambda c, carry: direction(nchunks - 1 - c, carry, True), h0)


def _rglru(xrg, grg, cw, cb, w, bias, lam):
    b, s, c = xrg.shape
    ct = w.shape[1]
    tc = _pick(s, 256)
    return pl.pallas_call(
        functools.partial(_rglru_body, tc=tc),
        grid=(b, c // ct),
        in_specs=[
            pl.BlockSpec((1, s, ct), lambda bi, j: (bi, 0, j)),
            pl.BlockSpec((1, s, ct), lambda bi, j: (bi, 0, j)),
            pl.BlockSpec((CONV_W, ct), lambda bi, j: (0, j)),
            pl.BlockSpec((1, ct), lambda bi, j: (0, j)),
            pl.BlockSpec((1, ct, 4 * ct), lambda bi, j: (j, 0, 0)),
            pl.BlockSpec((1, 1, 4 * ct), lambda bi, j: (j, 0, 0)),
            pl.BlockSpec((2, ct), lambda bi, j: (0, j)),
        ],
        out_specs=pl.BlockSpec((1, s, ct), lambda bi, j: (bi, 0, j)),
        out_shape=jax.ShapeDtypeStruct((b, s, c), F32),
        scratch_shapes=[pltpu.VMEM((s + 2 * V7X_SUBLANES, ct), F32), pltpu.VMEM((s, ct), F32)],
        compiler_params=pltpu.CompilerParams(
            dimension_semantics=("parallel", "parallel"),
            vmem_limit_bytes=V7X_VMEM_LIMIT_BYTES),
        name="rglru",
    )(xrg, grg, cw, cb, w, bias, lam)


def _attn_body(q_ref, k_ref, v_ref, o_ref, *, tk):
    tq = q_ref.shape[2]
    s = k_ref.shape[2]
    q = q_ref[0, 0]

    def step(j, carry):
        m, l, acc = carry
        k0 = pl.multiple_of(j * tk, tk)
        k = k_ref[0, 0, pl.ds(k0, tk), :]
        v = v_ref[0, 0, pl.ds(k0, tk), :]
        sc = lax.dot_general(q, k, (((1,), (1,)), ((), ())), preferred_element_type=F32)
        m_new = jnp.maximum(m, jnp.max(sc, axis=-1, keepdims=True))
        p = jnp.exp(sc - m_new)
        alpha = jnp.exp(m - m_new)
        l = alpha * l + jnp.sum(p, axis=-1, keepdims=True)
        acc = alpha * acc + jnp.dot(p.astype(BF16), v, preferred_element_type=F32)
        return m_new, l, acc

    init = (jnp.full((tq, 1), -jnp.inf, F32), jnp.zeros((tq, 1), F32), jnp.zeros((tq, V_DIM), F32))
    _, l, acc = lax.fori_loop(0, s // tk, step, init)
    o_ref[0] = acc / l


def _attention(q, k, v):
    b, nh, s, _ = q.shape
    tq = _pick(s, 256)
    tk = _pick(s, 512)
    return pl.pallas_call(
        functools.partial(_attn_body, tk=tk),
        grid=(b, nh, s // tq),
        in_specs=[
            pl.BlockSpec((1, 1, tq, HEAD_PAD), lambda bi, h, i: (bi, h, i, 0)),
            pl.BlockSpec((1, 1, s, HEAD_PAD), lambda bi, h, i: (bi, h, 0, 0)),
            pl.BlockSpec((1, 1, s, V_DIM), lambda bi, h, i: (bi, h, 0, 0)),
        ],
        out_specs=pl.BlockSpec((1, tq, V_DIM), lambda bi, h, i: (bi, i, h)),
        out_shape=jax.ShapeDtypeStruct((b, s, nh * V_DIM), F32),
        compiler_params=pltpu.CompilerParams(
            dimension_semantics=("parallel", "parallel", "parallel"),
            vmem_limit_bytes=V7X_VMEM_LIMIT_BYTES),
        name="attention",
    )(q, k, v)


def _mixer_out_body(x_ref, yr_ref, ya_ref, gr_ref, ga_ref, w_ref, o_ref):
    y = jnp.concatenate([_rms(yr_ref[...], gr_ref[...]).astype(BF16),
                         _rms(ya_ref[...], ga_ref[...]).astype(BF16)], axis=-1)
    o_ref[...] = x_ref[...] + jnp.dot(y, w_ref[...], preferred_element_type=F32)


def _mixer_out(x, yr, ya, gr, ga, w):
    t, d = x.shape
    tm = _pick(t, 512)
    return pl.pallas_call(
        _mixer_out_body,
        grid=(t // tm,),
        in_specs=[
            pl.BlockSpec((tm, d), lambda i: (i, 0)),
            pl.BlockSpec((tm, yr.shape[1]), lambda i: (i, 0)),
            pl.BlockSpec((tm, ya.shape[1]), lambda i: (i, 0)),
            _const_spec((1, yr.shape[1])),
            _const_spec((1, ya.shape[1])),
            _const_spec(w.shape),
        ],
        out_specs=pl.BlockSpec((tm, d), lambda i: (i, 0)),
        out_shape=jax.ShapeDtypeStruct((t, d), F32),
        compiler_params=pltpu.CompilerParams(
            dimension_semantics=("parallel",),
            vmem_limit_bytes=V7X_VMEM_LIMIT_BYTES),
        name="mixer_out",
    )(x, yr, ya, gr, ga, w)


def _rope_tables(seq):
    inv = 1.0 / (ROPE_THETA ** (jnp.arange(0, QK_ROPE, 2, dtype=F32) / QK_ROPE))
    ang = jnp.arange(seq, dtype=F32)[:, None] * inv[None, :]
    cos, sin = jnp.cos(ang), jnp.sin(ang)
    scale = 1.0 / math.sqrt(QK_NOPE + QK_ROPE)
    tq = scale * jnp.concatenate([jnp.ones((seq, QK_NOPE), F32), cos, cos, -sin, sin], axis=1)
    tkc = jnp.concatenate([cos, cos, cos, cos], axis=1)
    tks = jnp.concatenate([-sin, sin, -sin, sin], axis=1)
    return tq, tkc, tks


def _layout_w_in(w_in):
    ko = 2 * RG_WIDTH + Q_LORA + KV_LORA
    k1 = w_in[:, ko:ko + ROPE_HALF]
    k2 = w_in[:, ko + ROPE_HALF:ko + QK_ROPE]
    return jnp.concatenate([w_in[:, :ko], k1, k2, k1, k2, k2, k1, k2, k1], axis=1).astype(BF16)


def _layout_w_q(w_q_b):
    w = w_q_b.reshape(Q_LORA, MLA_HEADS, QK_NOPE + QK_ROPE)
    x1 = w[:, :, QK_NOPE:QK_NOPE + ROPE_HALF]
    x2 = w[:, :, QK_NOPE + ROPE_HALF:]
    w = jnp.concatenate([w[:, :, :QK_NOPE], x1, x2, x2, x1], axis=-1)
    return w.reshape(Q_LORA, MLA_HEADS * HEAD_PAD).astype(BF16)


def _layout_w_kv(w_kv_b):
    w = w_kv_b.reshape(KV_LORA, MLA_HEADS, QK_NOPE + V_DIM)
    return jnp.concatenate([w[:, :, :QK_NOPE].reshape(KV_LORA, -1),
                            w[:, :, QK_NOPE:].reshape(KV_LORA, -1)], axis=1).astype(BF16)


def _layout_rg_gates(rg_w_a, rg_b_a, rg_w_x, rg_b_x, ct):
    g = ct // RG_BLOCK_W
    nt = RG_WIDTH // ct
    eye = jnp.eye(g, dtype=F32)

    def tiles(w):
        w = w.reshape(nt, g, RG_BLOCK_W, RG_BLOCK_W)
        return jnp.einsum("tgij,gk->tgikj", w, eye).reshape(nt, ct, ct)

    w = jnp.concatenate([tiles(rg_w_a[0]), tiles(rg_w_x[0]), tiles(rg_w_a[1]), tiles(rg_w_x[1])], axis=-1)
    bias = jnp.concatenate([rg_b_a[0].reshape(nt, 1, ct), rg_b_x[0].reshape(nt, 1, ct),
                            rg_b_a[1].reshape(nt, 1, ct), rg_b_x[1].reshape(nt, 1, ct)], axis=-1)
    return w.astype(BF16), bias


def _trunk(x, p):
    b, s, d = x.shape
    t = b * s
    row = lambda v: v.reshape(1, -1)
    x0 = x.reshape(t, d)
    x1 = _ffn(x0, row(p["ffn1_norm"]), p["ffn1_wg"], p["ffn1_wu"], p["ffn1_wd"], row(p["ffn1_norm"]),
              final_norm=False)

    tq, tkc, tks = _rope_tables(s)
    xrg, grg, q, k, v = _mixer_in(x1.reshape(b, s, d), row(p["mix_norm"]), p["w_in"], row(p["q_a_norm"]),
                                  p["w_q"], row(p["kv_a_norm"]), p["w_kv"], tq, tkc, tks)
    ct = 256 if s <= 4096 else 128
    gw, gb = p["rg_gates"][ct]
    y_rg = _rglru(xrg, grg, p["conv_w"], row(p["conv_b"]), gw, gb, p["rg_lambda"])
    y_att = _attention(q, k, v)
    x2 = _mixer_out(x1, y_rg.reshape(t, -1), y_att.reshape(t, -1), row(p["rg_out_norm"]),
                    row(p["attn_out_norm"]), p["w_out"])
    y = _ffn(x2, row(p["ffn2_norm"]), p["ffn2_wg"], p["ffn2_wu"], p["ffn2_wd"], row(p["final_norm"]),
             final_norm=True)
    return y.reshape(b, s, d)


def kernel(x_prompt, x_sample, ffn1_norm, ffn1_w_gate, ffn1_w_up, ffn1_w_down, mix_norm, w_in, conv_w, conv_b, rg_w_a, rg_b_a, rg_w_x, rg_b_x, rg_lambda, q_a_norm, w_q_b, kv_a_norm, w_kv_b, rg_out_norm, attn_out_norm, w_out, ffn2_norm, ffn2_w_gate, ffn2_w_up, ffn2_w_down, final_norm):
    depth = ffn1_norm.shape[0]

    def layer_params(l):
        return {
            "ffn1_norm": ffn1_norm[l], "ffn1_wg": ffn1_w_gate[l].astype(BF16),
            "ffn1_wu": ffn1_w_up[l].astype(BF16), "ffn1_wd": ffn1_w_down[l].astype(BF16),
            "mix_norm": mix_norm[l], "w_in": _layout_w_in(w_in[l]),
            "conv_w": conv_w[l], "conv_b": conv_b[l],
            "rg_gates": {ct: _layout_rg_gates(rg_w_a[l], rg_b_a[l], rg_w_x[l], rg_b_x[l], ct)
                         for ct in (128, 256)},
            "rg_lambda": rg_lambda[l],
            "q_a_norm": q_a_norm[l], "w_q": _layout_w_q(w_q_b[l]),
            "kv_a_norm": kv_a_norm[l], "w_kv": _layout_w_kv(w_kv_b[l]),
            "rg_out_norm": rg_out_norm[l], "attn_out_norm": attn_out_norm[l],
            "w_out": w_out[l].astype(BF16),
            "ffn2_norm": ffn2_norm[l], "ffn2_wg": ffn2_w_gate[l].astype(BF16),
            "ffn2_wu": ffn2_w_up[l].astype(BF16), "ffn2_wd": ffn2_w_down[l].astype(BF16),
            "final_norm": final_norm,
        }

    assert depth == 1, "the final rmsnorm is fused into the last layer's second FFN"
    p = layer_params(0)
    return (_trunk(x_prompt, p), _trunk(x_sample, p))
```

```python
import functools
import math

import jax
import jax.numpy as jnp
from jax import lax
from jax.experimental import pallas as pl
from jax.experimental.pallas import tpu as pltpu

F32 = jnp.float32
BF16 = jnp.bfloat16

NORM_EPS = 1e-6
RG_WIDTH = 1024
RG_BLOCKS = 16
RG_BLOCK_W = RG_WIDTH // RG_BLOCKS
RG_C = 8.0
CONV_W = 4
CONV_PAD_L = 2
MLA_HEADS = 8
Q_LORA = 512
KV_LORA = 256
QK_NOPE = 128
QK_ROPE = 64
V_DIM = 128
ROPE_THETA = 10000.0
ROPE_HALF = QK_ROPE // 2
HEAD_PAD = 256

V7X_SUBLANES = 8
V7X_LANES = 128
V7X_VMEM_LIMIT_BYTES = 60000 * 1024


def _pick(dim, pref):
    t = min(dim, pref)
    while dim % t:
        t //= 2
    return t


def _rms(x, g):
    return x * lax.rsqrt(jnp.mean(x * x, axis=-1, keepdims=True) + NORM_EPS) * g


def _const_spec(shape):
    nd = len(shape)
    return pl.BlockSpec(shape, lambda *_: (0,) * nd, pipeline_mode=pl.Buffered(1))


def _ffn_body(x_ref, g_ref, wg_ref, wu_ref, wd_ref, pg_ref, o_ref, xn_ref, *, final_norm):
    f = pl.program_id(1)

    @pl.when(f == 0)
    def _():
        x = x_ref[...]
        xn_ref[...] = _rms(x, g_ref[...]).astype(BF16)
        o_ref[...] = x

    xn = xn_ref[...]
    gate = jnp.dot(xn, wg_ref[...], preferred_element_type=F32)
    up = jnp.dot(xn, wu_ref[...], preferred_element_type=F32)
    h = (gate * jax.nn.sigmoid(gate)) * (0.5 * up)
    o_ref[...] += jnp.dot(h.astype(BF16), wd_ref[...], preferred_element_type=F32)

    if final_norm:
        @pl.when(f == pl.num_programs(1) - 1)
        def _():
            o_ref[...] = _rms(o_ref[...], pg_ref[...])


def _ffn(x, g, wg, wu, wd, pg, *, final_norm):
    t, d = x.shape
    dff = wg.shape[1]
    tm = _pick(t, 512)
    tf = _pick(dff, 512)
    return pl.pallas_call(
        functools.partial(_ffn_body, final_norm=final_norm),
        grid=(t // tm, dff // tf),
        in_specs=[
            pl.BlockSpec((tm, d), lambda i, f: (i, 0)),
            _const_spec((1, d)),
            pl.BlockSpec((d, tf), lambda i, f: (0, f)),
            pl.BlockSpec((d, tf), lambda i, f: (0, f)),
            pl.BlockSpec((tf, d), lambda i, f: (f, 0)),
            _const_spec((1, d)),
        ],
        out_specs=pl.BlockSpec((tm, d), lambda i, f: (i, 0)),
        out_shape=jax.ShapeDtypeStruct((t, d), F32),
        scratch_shapes=[pltpu.VMEM((tm, d), BF16)],
        compiler_params=pltpu.CompilerParams(
            dimension_semantics=("parallel", "arbitrary"),
            vmem_limit_bytes=V7X_VMEM_LIMIT_BYTES),
        name="ffn",
    )(x, g, wg, wu, wd, pg)


def _mixer_in_body(x_ref, g_ref, win_ref, qg_ref, wqt_ref, kvg_ref, wk_ref, wvt_ref, tqt_ref, tkc_ref, tks_ref,
                   xrg_ref, grg_ref, qt_ref, k_ref, vt_ref):
    hn = _rms(x_ref[0], g_ref[...]).astype(BF16)
    z = jnp.dot(hn, win_ref[...], preferred_element_type=F32)
    o = 0
    xrg_ref[0] = z[:, o:o + RG_WIDTH]
    o += RG_WIDTH
    grg_ref[0] = z[:, o:o + RG_WIDTH]
    o += RG_WIDTH
    qc = z[:, o:o + Q_LORA]
    o += Q_LORA
    kvc = z[:, o:o + KV_LORA]
    o += KV_LORA
    ka = z[:, o:o + V7X_LANES]
    kb = z[:, o + V7X_LANES:o + 2 * V7X_LANES]

    nt = (((1,), (1,)), ((), ()))
    qn = _rms(qc, qg_ref[...]).astype(BF16)
    kvn = _rms(kvc, kvg_ref[...]).astype(BF16)
    qt = lax.dot_general(wqt_ref[...], qn, nt, preferred_element_type=F32)
    vt = lax.dot_general(wvt_ref[...], kvn, nt, preferred_element_type=F32)
    kn = jnp.dot(kvn, wk_ref[...], preferred_element_type=F32)
    kr = (ka * tkc_ref[...] + kb * tks_ref[...]).astype(BF16)
    tqt = tqt_ref[...]
    for h in range(MLA_HEADS):
        qt_ref[0, h] = (qt[h * HEAD_PAD:(h + 1) * HEAD_PAD, :] * tqt).astype(BF16)
        k_ref[0, h, :, 0:QK_NOPE] = kn[:, h * QK_NOPE:(h + 1) * QK_NOPE].astype(BF16)
        k_ref[0, h, :, QK_NOPE:HEAD_PAD] = kr
        vt_ref[0, h] = vt[h * V_DIM:(h + 1) * V_DIM, :].astype(BF16)


def _mixer_in(x, g, win, qg, wqt, kvg, wk, wvt, tqt, tkc, tks):
    b, s, d = x.shape
    tm = _pick(s, 512)
    din = win.shape[1]
    nh = MLA_HEADS
    return pl.pallas_call(
        _mixer_in_body,
        grid=(b, s // tm),
        in_specs=[
            pl.BlockSpec((1, tm, d), lambda bi, i: (bi, i, 0)),
            _const_spec((1, d)),
            _const_spec((d, din)),
            _const_spec((1, Q_LORA)),
            _const_spec((nh * HEAD_PAD, Q_LORA)),
            _const_spec((1, KV_LORA)),
            _const_spec((KV_LORA, nh * QK_NOPE)),
            _const_spec((nh * V_DIM, KV_LORA)),
            pl.BlockSpec((HEAD_PAD, tm), lambda bi, i: (0, i)),
            pl.BlockSpec((tm, V7X_LANES), lambda bi, i: (i, 0)),
            pl.BlockSpec((tm, V7X_LANES), lambda bi, i: (i, 0)),
        ],
        out_specs=[
            pl.BlockSpec((1, tm, RG_WIDTH), lambda bi, i: (bi, i, 0)),
            pl.BlockSpec((1, tm, RG_WIDTH), lambda bi, i: (bi, i, 0)),
            pl.BlockSpec((1, nh, HEAD_PAD, tm), lambda bi, i: (bi, 0, 0, i)),
            pl.BlockSpec((1, nh, tm, HEAD_PAD), lambda bi, i: (bi, 0, i, 0)),
            pl.BlockSpec((1, nh, V_DIM, tm), lambda bi, i: (bi, 0, 0, i)),
        ],
        out_shape=[
            jax.ShapeDtypeStruct((b, s, RG_WIDTH), F32),
            jax.ShapeDtypeStruct((b, s, RG_WIDTH), F32),
            jax.ShapeDtypeStruct((b, nh, HEAD_PAD, s), BF16),
            jax.ShapeDtypeStruct((b, nh, s, HEAD_PAD), BF16),
            jax.ShapeDtypeStruct((b, nh, V_DIM, s), BF16),
        ],
        compiler_params=pltpu.CompilerParams(
            dimension_semantics=("parallel", "parallel"),
            vmem_limit_bytes=V7X_VMEM_LIMIT_BYTES),
        name="mixer_in",
    )(x, g, win, qg, wqt, kvg, wk, wvt, tqt, tkc, tks)


def _scan_block(a, b, rows, reverse):
    n = V7X_SUBLANES
    for s in (1, 2, 4):
        if reverse:
            a_s = pltpu.roll(a, n - s, 0)
            b_s = pltpu.roll(b, n - s, 0)
            valid = rows < n - s
        else:
            a_s = pltpu.roll(a, s, 0)
            b_s = pltpu.roll(b, s, 0)
            valid = rows >= s
        b = jnp.where(valid, a * b_s + b, b)
        a = jnp.where(valid, a * a_s, a)
    return a, b


def _rglru_body(x_ref, g_ref, cw_ref, cb_ref, w_ref, bias_ref, lam_ref, y_ref, xpad_ref, hf_ref, *, tc):
    s, ct = x_ref.shape[1], x_ref.shape[2]
    n = V7X_SUBLANES
    nchunks = s // tc
    halo = n

    zeros = jnp.zeros((halo, ct), F32)
    xpad_ref[0:halo, :] = zeros
    xpad_ref[halo + s:halo + s + halo, :] = zeros
    xpad_ref[halo:halo + s, :] = x_ref[0]

    cw = cw_ref[...]
    cb = cb_ref[...]
    sp = jax.nn.softplus(-lam_ref[...])
    rows = lax.broadcasted_iota(jnp.int32, (n, ct), 0)

    def direction(c, carry, reverse):
        t0 = pl.multiple_of(c * tc, tc)
        blk = xpad_ref[pl.ds(t0, tc + 2 * halo), :]
        xc = cb
        for k in range(CONV_W):
            o = halo - CONV_PAD_L + k
            xc = xc + cw[k:k + 1, :] * blk[o:o + tc, :]
        d = 1 if reverse else 0
        gates = jnp.dot(xc.astype(BF16), w_ref[0, :, 2 * d * ct:2 * (d + 1) * ct],
                        preferred_element_type=F32) + bias_ref[0, :, 2 * d * ct:2 * (d + 1) * ct]
        r = jax.nn.sigmoid(gates[:, :ct])
        i = jax.nn.sigmoid(gates[:, ct:])
        log_a = (-RG_C) * r * sp[d:d + 1, :]
        a = jnp.exp(log_a)
        bterm = jnp.sqrt(1.0 - jnp.exp(2.0 * log_a)) * (i * xc)
        nb = tc // n
        order = range(nb - 1, -1, -1) if reverse else range(nb)
        for j in order:
            a_c, h = _scan_block(a[j * n:(j + 1) * n, :], bterm[j * n:(j + 1) * n, :], rows, reverse)
            h = h + a_c * carry
            carry = h[0:1, :] if reverse else h[n - 1:n, :]
            tj = pl.multiple_of(t0 + j * n, n)
            if reverse:
                hsum = hf_ref[pl.ds(tj, n), :] + h
                y_ref[0, pl.ds(tj, n), :] = hsum * jax.nn.gelu(g_ref[0, pl.ds(tj, n), :])
            else:
                hf_ref[pl.ds(tj, n), :] = h
        return carry

    h0 = jnp.zeros((1, ct), F32)
    lax.fori_loop(0, nchunks, lambda c, carry: direction(c, carry, False), h0)
    lax.fori_loop(0, nchunks, lambda c, carry: direction(nchunks - 1 - c, carry, True), h0)


def _rglru(xrg, grg, cw, cb, w, bias, lam):
    b, s, c = xrg.shape
    ct = w.shape[1]
    tc = _pick(s, 256)
    return pl.pallas_call(
        functools.partial(_rglru_body, tc=tc),
        grid=(b, c // ct),
        in_specs=[
            pl.BlockSpec((1, s, ct), lambda bi, j: (bi, 0, j)),
            pl.BlockSpec((1, s, ct), lambda bi, j: (bi, 0, j)),
            pl.BlockSpec((CONV_W, ct), lambda bi, j: (0, j)),
            pl.BlockSpec((1, ct), lambda bi, j: (0, j)),
            pl.BlockSpec((1, ct, 4 * ct), lambda bi, j: (j, 0, 0)),
            pl.BlockSpec((1, 1, 4 * ct), lambda bi, j: (j, 0, 0)),
            pl.BlockSpec((2, ct), lambda bi, j: (0, j)),
        ],
        out_specs=pl.BlockSpec((1, s, ct), lambda bi, j: (bi, 0, j)),
        out_shape=jax.ShapeDtypeStruct((b, s, c), F32),
        scratch_shapes=[pltpu.VMEM((s + 2 * V7X_SUBLANES, ct), F32), pltpu.VMEM((s, ct), F32)],
        compiler_params=pltpu.CompilerParams(
            dimension_semantics=("parallel", "parallel"),
            vmem_limit_bytes=V7X_VMEM_LIMIT_BYTES),
        name="rglru",
    )(xrg, grg, cw, cb, w, bias, lam)


def _attn_body(qt_ref, k_ref, vt_ref, o_ref, *, tk):
    tq = qt_ref.shape[3]
    s = k_ref.shape[2]
    qt = qt_ref[0, 0]
    m = jnp.full((1, tq), -jnp.inf, F32)
    l = jnp.zeros((1, tq), F32)
    acc = jnp.zeros((V_DIM, tq), F32)
    nk = s // tk

    def scores(j):
        return jnp.dot(k_ref[0, 0, j * tk:(j + 1) * tk, :], qt, preferred_element_type=F32)

    st_next = scores(0)
    for j in range(nk):
        st = st_next
        if j + 1 < nk:
            st_next = scores(j + 1)
        m_new = jnp.maximum(m, jnp.max(st, axis=0, keepdims=True))
        p = jnp.exp2(st - m_new)
        alpha = jnp.exp2(m - m_new)
        l = alpha * l + jnp.sum(p, axis=0, keepdims=True)
        acc = alpha * acc + jnp.dot(vt_ref[0, 0, :, j * tk:(j + 1) * tk], p.astype(BF16),
                                    preferred_element_type=F32)
        m = m_new
    o_ref[0] = (acc / l).T


def _attention(qt, k, vt):
    b, nh, s, _ = k.shape
    tq = _pick(s, 1024)
    tk = _pick(s, 512)
    return pl.pallas_call(
        functools.partial(_attn_body, tk=tk),
        grid=(b, nh, s // tq),
        in_specs=[
            pl.BlockSpec((1, 1, HEAD_PAD, tq), lambda bi, h, i: (bi, h, 0, i)),
            pl.BlockSpec((1, 1, s, HEAD_PAD), lambda bi, h, i: (bi, h, 0, 0)),
            pl.BlockSpec((1, 1, V_DIM, s), lambda bi, h, i: (bi, h, 0, 0)),
        ],
        out_specs=pl.BlockSpec((1, tq, V_DIM), lambda bi, h, i: (bi, i, h)),
        out_shape=jax.ShapeDtypeStruct((b, s, nh * V_DIM), F32),
        compiler_params=pltpu.CompilerParams(
            dimension_semantics=("parallel", "parallel", "parallel"),
            vmem_limit_bytes=V7X_VMEM_LIMIT_BYTES),
        name="attention",
    )(qt, k, vt)


def _mixer_out_body(x_ref, yr_ref, ya_ref, gr_ref, ga_ref, w_ref, o_ref):
    y = jnp.concatenate([_rms(yr_ref[...], gr_ref[...]).astype(BF16),
                         _rms(ya_ref[...], ga_ref[...]).astype(BF16)], axis=-1)
    o_ref[...] = x_ref[...] + jnp.dot(y, w_ref[...], preferred_element_type=F32)


def _mixer_out(x, yr, ya, gr, ga, w):
    t, d = x.shape
    tm = _pick(t, 512)
    return pl.pallas_call(
        _mixer_out_body,
        grid=(t // tm,),
        in_specs=[
            pl.BlockSpec((tm, d), lambda i: (i, 0)),
            pl.BlockSpec((tm, yr.shape[1]), lambda i: (i, 0)),
            pl.BlockSpec((tm, ya.shape[1]), lambda i: (i, 0)),
            _const_spec((1, yr.shape[1])),
            _const_spec((1, ya.shape[1])),
            _const_spec(w.shape),
        ],
        out_specs=pl.BlockSpec((tm, d), lambda i: (i, 0)),
        out_shape=jax.ShapeDtypeStruct((t, d), F32),
        compiler_params=pltpu.CompilerParams(
            dimension_semantics=("parallel",),
            vmem_limit_bytes=V7X_VMEM_LIMIT_BYTES),
        name="mixer_out",
    )(x, yr, ya, gr, ga, w)


def _rope_tables(seq):
    inv = 1.0 / (ROPE_THETA ** (jnp.arange(0, QK_ROPE, 2, dtype=F32) / QK_ROPE))
    ang = jnp.arange(seq, dtype=F32)[:, None] * inv[None, :]
    cos, sin = jnp.cos(ang), jnp.sin(ang)
    scale = math.log2(math.e) / math.sqrt(QK_NOPE + QK_ROPE)
    tqt = (scale * jnp.concatenate([jnp.ones((seq, QK_NOPE), F32), cos, cos, -sin, sin], axis=1)).T
    tkc = jnp.concatenate([cos, cos, cos, cos], axis=1)
    tks = jnp.concatenate([-sin, sin, -sin, sin], axis=1)
    return tqt, tkc, tks


def _layout_w_in(w_in):
    ko = 2 * RG_WIDTH + Q_LORA + KV_LORA
    k1 = w_in[:, ko:ko + ROPE_HALF]
    k2 = w_in[:, ko + ROPE_HALF:ko + QK_ROPE]
    return jnp.concatenate([w_in[:, :ko], k1, k2, k1, k2, k2, k1, k2, k1], axis=1).astype(BF16)


def _layout_w_q(w_q_b):
    w = w_q_b.reshape(Q_LORA, MLA_HEADS, QK_NOPE + QK_ROPE)
    x1 = w[:, :, QK_NOPE:QK_NOPE + ROPE_HALF]
    x2 = w[:, :, QK_NOPE + ROPE_HALF:]
    w = jnp.concatenate([w[:, :, :QK_NOPE], x1, x2, x2, x1], axis=-1)
    return w.reshape(Q_LORA, MLA_HEADS * HEAD_PAD).T.astype(BF16)


def _layout_w_kv(w_kv_b):
    w = w_kv_b.reshape(KV_LORA, MLA_HEADS, QK_NOPE + V_DIM)
    wk = w[:, :, :QK_NOPE].reshape(KV_LORA, -1)
    wvt = w[:, :, QK_NOPE:].reshape(KV_LORA, -1).T
    return wk.astype(BF16), wvt.astype(BF16)


def _layout_rg_gates(rg_w_a, rg_b_a, rg_w_x, rg_b_x, ct):
    g = ct // RG_BLOCK_W
    nt = RG_WIDTH // ct
    eye = jnp.eye(g, dtype=F32)

    def tiles(w):
        w = w.reshape(nt, g, RG_BLOCK_W, RG_BLOCK_W)
        return jnp.einsum("tgij,gk->tgikj", w, eye).reshape(nt, ct, ct)

    w = jnp.concatenate([tiles(rg_w_a[0]), tiles(rg_w_x[0]), tiles(rg_w_a[1]), tiles(rg_w_x[1])], axis=-1)
    bias = jnp.concatenate([rg_b_a[0].reshape(nt, 1, ct), rg_b_x[0].reshape(nt, 1, ct),
                            rg_b_a[1].reshape(nt, 1, ct), rg_b_x[1].reshape(nt, 1, ct)], axis=-1)
    return w.astype(BF16), bias


def _trunk(x, p):
    b, s, d = x.shape
    t = b * s
    row = lambda v: v.reshape(1, -1)
    x0 = x.reshape(t, d)
    x1 = _ffn(x0, row(p["ffn1_norm"]), p["ffn1_wg"], p["ffn1_wu"], p["ffn1_wd"], row(p["ffn1_norm"]),
              final_norm=False)

    tqt, tkc, tks = _rope_tables(s)
    xrg, grg, qt, k, vt = _mixer_in(x1.reshape(b, s, d), row(p["mix_norm"]), p["w_in"], row(p["q_a_norm"]),
                                    p["w_qt"], row(p["kv_a_norm"]), p["w_k"], p["w_vt"], tqt, tkc, tks)
    ct = 256 if s <= 4096 else 128
    gw, gb = p["rg_gates"][ct]
    y_rg = _rglru(xrg, grg, p["conv_w"], row(p["conv_b"]), gw, gb, p["rg_lambda"])
    y_att = _attention(qt, k, vt)
    x2 = _mixer_out(x1, y_rg.reshape(t, -1), y_att.reshape(t, -1), row(p["rg_out_norm"]),
                    row(p["attn_out_norm"]), p["w_out"])
    y = _ffn(x2, row(p["ffn2_norm"]), p["ffn2_wg"], p["ffn2_wu"], p["ffn2_wd"], row(p["final_norm"]),
             final_norm=True)
    return y.reshape(b, s, d)


def kernel(x_prompt, x_sample, ffn1_norm, ffn1_w_gate, ffn1_w_up, ffn1_w_down, mix_norm, w_in, conv_w, conv_b, rg_w_a, rg_b_a, rg_w_x, rg_b_x, rg_lambda, q_a_norm, w_q_b, kv_a_norm, w_kv_b, rg_out_norm, attn_out_norm, w_out, ffn2_norm, ffn2_w_gate, ffn2_w_up, ffn2_w_down, final_norm):
    depth = ffn1_norm.shape[0]

    def layer_params(l):
        w_k, w_vt = _layout_w_kv(w_kv_b[l])
        return {
            "ffn1_norm": ffn1_norm[l], "ffn1_wg": ffn1_w_gate[l].astype(BF16),
            "ffn1_wu": ffn1_w_up[l].astype(BF16), "ffn1_wd": ffn1_w_down[l].astype(BF16),
            "mix_norm": mix_norm[l], "w_in": _layout_w_in(w_in[l]),
            "conv_w": conv_w[l], "conv_b": conv_b[l],
            "rg_gates": {ct: _layout_rg_gates(rg_w_a[l], rg_b_a[l], rg_w_x[l], rg_b_x[l], ct)
                         for ct in (128, 256)},
            "rg_lambda": rg_lambda[l],
            "q_a_norm": q_a_norm[l], "w_qt": _layout_w_q(w_q_b[l]),
            "kv_a_norm": kv_a_norm[l], "w_k": w_k, "w_vt": w_vt,
            "rg_out_norm": rg_out_norm[l], "attn_out_norm": attn_out_norm[l],
            "w_out": w_out[l].astype(BF16),
            "ffn2_norm": ffn2_norm[l], "ffn2_wg": ffn2_w_gate[l].astype(BF16),
            "ffn2_wu": ffn2_w_up[l].astype(BF16), "ffn2_wd": ffn2_w_down[l].astype(BF16),
            "final_norm": final_norm,
        }

    assert depth == 1, "the final rmsnorm is fused into the last layer's second FFN"
    p = layer_params(0)
    return (_trunk(x_prompt, p), _trunk(x_sample, p))
```

```python
import functools
import math

import jax
import jax.numpy as jnp
from jax import lax
from jax.experimental import pallas as pl
from jax.experimental.pallas import tpu as pltpu

F32 = jnp.float32
BF16 = jnp.bfloat16

NORM_EPS = 1e-6
RG_WIDTH = 1024
RG_BLOCKS = 16
RG_BLOCK_W = RG_WIDTH // RG_BLOCKS
RG_C = 8.0
CONV_W = 4
CONV_PAD_L = 2
MLA_HEADS = 8
Q_LORA = 512
KV_LORA = 256
QK_NOPE = 128
QK_ROPE = 64
V_DIM = 128
ROPE_THETA = 10000.0
ROPE_HALF = QK_ROPE // 2
HEAD_PAD = 256

V7X_SUBLANES = 8
V7X_LANES = 128
V7X_VMEM_LIMIT_BYTES = 60000 * 1024


def _pick(dim, pref):
    t = min(dim, pref)
    while dim % t:
        t //= 2
    return t


def _rms(x, g):
    return x * lax.rsqrt(jnp.mean(x * x, axis=-1, keepdims=True) + NORM_EPS) * g


def _sigmoid(x):
    return 0.5 * jnp.tanh(0.5 * x) + 0.5


def _const_spec(shape):
    nd = len(shape)
    return pl.BlockSpec(shape, lambda *_: (0,) * nd, pipeline_mode=pl.Buffered(1))


def _ffn_body(x_ref, g_ref, wgu_ref, wd_ref, pg_ref, o_ref, xn_ref, *, final_norm):
    f = pl.program_id(1)
    tf = wd_ref.shape[0]

    def half_swiglu(xn):
        gu = jnp.dot(xn, wgu_ref[...], preferred_element_type=F32)
        gate, up = gu[:, :tf], gu[:, tf:]
        h = (gate * jax.nn.sigmoid(gate)) * (0.5 * up)
        return jnp.dot(h.astype(BF16), wd_ref[...], preferred_element_type=F32)

    @pl.when(f == 0)
    def _():
        x = x_ref[...]
        xn = _rms(x, g_ref[...]).astype(BF16)
        xn_ref[...] = xn
        o_ref[...] = x + half_swiglu(xn)

    @pl.when(f > 0)
    def _():
        o_ref[...] += half_swiglu(xn_ref[...])

    if final_norm:
        @pl.when(f == pl.num_programs(1) - 1)
        def _():
            o_ref[...] = _rms(o_ref[...], pg_ref[...])


FFN_HIDDEN_TILE = 512


def _ffn(x, g, wgu, wd, pg, *, final_norm):
    t, d = x.shape
    tf = _pick(wd.shape[0], FFN_HIDDEN_TILE)
    nf = wd.shape[0] // tf
    tm = _pick(t, 1024)
    return pl.pallas_call(
        functools.partial(_ffn_body, final_norm=final_norm),
        grid=(t // tm, nf),
        in_specs=[
            pl.BlockSpec((tm, d), lambda i, f: (i, 0)),
            _const_spec((1, d)),
            pl.BlockSpec((d, 2 * tf), lambda i, f: (0, f)),
            pl.BlockSpec((tf, d), lambda i, f: (f, 0)),
            _const_spec((1, d)),
        ],
        out_specs=pl.BlockSpec((tm, d), lambda i, f: (i, 0)),
        out_shape=jax.ShapeDtypeStruct((t, d), F32),
        scratch_shapes=[pltpu.VMEM((tm, d), BF16)],
        compiler_params=pltpu.CompilerParams(
            dimension_semantics=("parallel", "arbitrary"),
            vmem_limit_bytes=V7X_VMEM_LIMIT_BYTES),
        name="ffn",
    )(x, g, wgu, wd, pg)


def _mixer_in_body(x_ref, g_ref, win_ref, qg_ref, wqt_ref, kvg_ref, wk_ref, wvt_ref, tqt_ref, tkc_ref, tks_ref,
                   xrg_ref, grg_ref, qt_ref, k_ref, vt_ref):
    hn = _rms(x_ref[0], g_ref[...]).astype(BF16)
    z = jnp.dot(hn, win_ref[...], preferred_element_type=F32)
    o = 0
    xrg_ref[0] = z[:, o:o + RG_WIDTH]
    o += RG_WIDTH
    grg_ref[0] = z[:, o:o + RG_WIDTH]
    o += RG_WIDTH
    qc = z[:, o:o + Q_LORA]
    o += Q_LORA
    kvc = z[:, o:o + KV_LORA]
    o += KV_LORA
    ka = z[:, o:o + V7X_LANES]
    kb = z[:, o + V7X_LANES:o + 2 * V7X_LANES]

    nt = (((1,), (1,)), ((), ()))
    qn = _rms(qc, qg_ref[...]).astype(BF16)
    kvn = _rms(kvc, kvg_ref[...]).astype(BF16)
    qt = lax.dot_general(wqt_ref[...], qn, nt, preferred_element_type=F32)
    vt = lax.dot_general(wvt_ref[...], kvn, nt, preferred_element_type=F32)
    kn = jnp.dot(kvn, wk_ref[...], preferred_element_type=F32)
    kr = (ka * tkc_ref[...] + kb * tks_ref[...]).astype(BF16)
    tqt = tqt_ref[...]
    for h in range(MLA_HEADS):
        qt_ref[0, h] = (qt[h * HEAD_PAD:(h + 1) * HEAD_PAD, :] * tqt).astype(BF16)
        k_ref[0, h, :, 0:QK_NOPE] = kn[:, h * QK_NOPE:(h + 1) * QK_NOPE].astype(BF16)
        k_ref[0, h, :, QK_NOPE:HEAD_PAD] = kr
        vt_ref[0, h] = vt[h * V_DIM:(h + 1) * V_DIM, :].astype(BF16)


def _mixer_in(x, g, win, qg, wqt, kvg, wk, wvt, tqt, tkc, tks):
    b, s, d = x.shape
    tm = _pick(s, 512)
    din = win.shape[1]
    nh = MLA_HEADS
    return pl.pallas_call(
        _mixer_in_body,
        grid=(b, s // tm),
        in_specs=[
            pl.BlockSpec((1, tm, d), lambda bi, i: (bi, i, 0)),
            _const_spec((1, d)),
            _const_spec((d, din)),
            _const_spec((1, Q_LORA)),
            _const_spec((nh * HEAD_PAD, Q_LORA)),
            _const_spec((1, KV_LORA)),
            _const_spec((KV_LORA, nh * QK_NOPE)),
            _const_spec((nh * V_DIM, KV_LORA)),
            pl.BlockSpec((HEAD_PAD, tm), lambda bi, i: (0, i)),
            pl.BlockSpec((tm, V7X_LANES), lambda bi, i: (i, 0)),
            pl.BlockSpec((tm, V7X_LANES), lambda bi, i: (i, 0)),
        ],
        out_specs=[
            pl.BlockSpec((1, tm, RG_WIDTH), lambda bi, i: (bi, i, 0)),
            pl.BlockSpec((1, tm, RG_WIDTH), lambda bi, i: (bi, i, 0)),
            pl.BlockSpec((1, nh, HEAD_PAD, tm), lambda bi, i: (bi, 0, 0, i)),
            pl.BlockSpec((1, nh, tm, HEAD_PAD), lambda bi, i: (bi, 0, i, 0)),
            pl.BlockSpec((1, nh, V_DIM, tm), lambda bi, i: (bi, 0, 0, i)),
        ],
        out_shape=[
            jax.ShapeDtypeStruct((b, s, RG_WIDTH), F32),
            jax.ShapeDtypeStruct((b, s, RG_WIDTH), F32),
            jax.ShapeDtypeStruct((b, nh, HEAD_PAD, s), BF16),
            jax.ShapeDtypeStruct((b, nh, s, HEAD_PAD), BF16),
            jax.ShapeDtypeStruct((b, nh, V_DIM, s), BF16),
        ],
        compiler_params=pltpu.CompilerParams(
            dimension_semantics=("parallel", "parallel"),
            vmem_limit_bytes=V7X_VMEM_LIMIT_BYTES),
        name="mixer_in",
    )(x, g, win, qg, wqt, kvg, wk, wvt, tqt, tkc, tks)


RG_SEGMENTS = V7X_SUBLANES
RG_SEG_PAD = V7X_SUBLANES


def _rglru_body(x_ref, g_ref, cw_ref, cb_ref, w_ref, bias_ref, lam_ref, y_ref,
                xpad_ref, af_ref, bf_ref, ab_ref, bb_ref, cf_ref, cr_ref, *, tc):
    s, ct = x_ref.shape[1], x_ref.shape[2]
    n = V7X_SUBLANES
    nl = V7X_LANES
    nslab = ct // nl
    seg_len = s // RG_SEGMENTS
    pitch = seg_len + RG_SEG_PAD
    nchunks = s // tc
    halo = n

    zeros = jnp.zeros((halo, ct), F32)
    xpad_ref[0:halo, :] = zeros
    xpad_ref[halo + s:halo + s + halo, :] = zeros
    xpad_ref[halo:halo + s, :] = x_ref[0]

    cw = cw_ref[...]
    cb = cb_ref[...]
    decay = (-RG_C * math.log2(math.e)) * jax.nn.softplus(-lam_ref[...])
    bias = bias_ref[0]

    def seg_row(t0):
        seg = t0 // seg_len
        return seg, pl.multiple_of(seg * pitch + (t0 - seg * seg_len), n)

    def phase1(c, _):
        t0 = pl.multiple_of(c * tc, tc)
        blk = xpad_ref[pl.ds(t0, tc + 2 * halo), :]
        xc = cb
        for k in range(CONV_W):
            o = halo - CONV_PAD_L + k
            xc = xc + cw[k:k + 1, :] * blk[o:o + tc, :]
        gates = jnp.dot(xc.astype(BF16), w_ref[0], preferred_element_type=F32) + bias
        _, r0 = seg_row(t0)
        for d, (a_ref, b_ref) in enumerate(((af_ref, bf_ref), (ab_ref, bb_ref))):
            r = _sigmoid(gates[:, 2 * d * ct:(2 * d + 1) * ct])
            i = _sigmoid(gates[:, (2 * d + 1) * ct:(2 * d + 2) * ct])
            a = jnp.exp2(r * decay[d:d + 1, :])
            y = 1.0 - a * a
            bterm = (y * lax.rsqrt(jnp.maximum(y, 1e-30))) * (i * xc)
            for sl in range(nslab):
                a_ref[sl, pl.ds(r0, tc), :] = a[:, sl * nl:(sl + 1) * nl]
                b_ref[sl, pl.ds(r0, tc), :] = bterm[:, sl * nl:(sl + 1) * nl]
        return 0

    lax.fori_loop(0, nchunks, phase1, 0)

    unroll = n // nslab

    def scan_group(tls, a_ref, b_ref, state):
        rows = [pl.ds(tl, RG_SEGMENTS, stride=pitch) for tl in tls]
        vals = [[(a_ref[sl, r, :], b_ref[sl, r, :]) for r in rows] for sl in range(nslab)]
        outs = []
        for sl in range(nslab):
            acum, h = state[sl]
            res = []
            for a, b in vals[sl]:
                acum = a * acum
                h = a * h + b
                res.append((acum, h))
            state[sl] = (acum, h)
            outs.append(res)
        for sl in range(nslab):
            for r, (acum, h) in zip(rows, outs[sl]):
                a_ref[sl, r, :] = acum
                b_ref[sl, r, :] = h

    ones = jnp.ones((RG_SEGMENTS, nl), F32)
    zero = jnp.zeros((RG_SEGMENTS, nl), F32)

    def phase2(it, carry):
        fwd = [carry[2 * sl] for sl in range(nslab)]
        rev = [carry[2 * sl + 1] for sl in range(nslab)]
        t0 = it * unroll
        scan_group([t0 + u for u in range(unroll)], af_ref, bf_ref, fwd)
        scan_group([seg_len - 1 - t0 - u for u in range(unroll)], ab_ref, bb_ref, rev)
        return tuple(x for sl in range(nslab) for x in (fwd[sl], rev[sl]))

    ends = lax.fori_loop(0, seg_len // unroll, phase2, tuple((ones, zero) for _ in range(2 * nslab)))

    seg_id = lax.broadcasted_iota(jnp.int32, (RG_SEGMENTS, nl), 0)
    for sl in range(nslab):
        for d, c_ref in enumerate((cf_ref, cr_ref)):
            acum, h = ends[2 * sl + d]
            carry = zero
            for k in range(1, RG_SEGMENTS):
                full = h + acum * carry
                if d == 0:
                    carry = jnp.where(seg_id == k, pltpu.roll(full, 1, 0), carry)
                else:
                    kk = RG_SEGMENTS - 1 - k
                    carry = jnp.where(seg_id == kk, pltpu.roll(full, RG_SEGMENTS - 1, 0), carry)
            c_ref[:, sl * nl:(sl + 1) * nl] = carry

    def phase3(c, _):
        t0 = pl.multiple_of(c * tc, tc)
        seg, r0 = seg_row(t0)
        cf = cf_ref[pl.ds(seg, 1), :]
        cr = cr_ref[pl.ds(seg, 1), :]
        for sl in range(nslab):
            lanes = slice(sl * nl, (sl + 1) * nl)
            hf = bf_ref[sl, pl.ds(r0, tc), :] + af_ref[sl, pl.ds(r0, tc), :] * cf[:, lanes]
            hr = bb_ref[sl, pl.ds(r0, tc), :] + ab_ref[sl, pl.ds(r0, tc), :] * cr[:, lanes]
            y_ref[0, pl.ds(t0, tc), lanes] = (hf + hr) * jax.nn.gelu(g_ref[0, pl.ds(t0, tc), lanes])
        return 0

    lax.fori_loop(0, nchunks, phase3, 0)


def _rglru(xrg, grg, cw, cb, w, bias, lam):
    b, s, c = xrg.shape
    ct = w.shape[1]
    seg_len = s // RG_SEGMENTS
    tc = _pick(seg_len, 256)
    seg_rows = RG_SEGMENTS * (seg_len + RG_SEG_PAD)
    seg_scratch = pltpu.VMEM((ct // V7X_LANES, seg_rows, V7X_LANES), F32)
    return pl.pallas_call(
        functools.partial(_rglru_body, tc=tc),
        grid=(b, c // ct),
        in_specs=[
            pl.BlockSpec((1, s, ct), lambda bi, j: (bi, 0, j)),
            pl.BlockSpec((1, s, ct), lambda bi, j: (bi, 0, j)),
            pl.BlockSpec((CONV_W, ct), lambda bi, j: (0, j)),
            pl.BlockSpec((1, ct), lambda bi, j: (0, j)),
            pl.BlockSpec((1, ct, 4 * ct), lambda bi, j: (j, 0, 0)),
            pl.BlockSpec((1, 1, 4 * ct), lambda bi, j: (j, 0, 0)),
            pl.BlockSpec((2, ct), lambda bi, j: (0, j)),
        ],
        out_specs=pl.BlockSpec((1, s, ct), lambda bi, j: (bi, 0, j)),
        out_shape=jax.ShapeDtypeStruct((b, s, c), F32),
        scratch_shapes=[pltpu.VMEM((s + 2 * V7X_SUBLANES, ct), F32),
                        seg_scratch, seg_scratch, seg_scratch, seg_scratch,
                        pltpu.VMEM((RG_SEGMENTS, ct), F32), pltpu.VMEM((RG_SEGMENTS, ct), F32)],
        compiler_params=pltpu.CompilerParams(
            dimension_semantics=("parallel", "parallel"),
            vmem_limit_bytes=V7X_VMEM_LIMIT_BYTES),
        name="rglru",
    )(xrg, grg, cw, cb, w, bias, lam)


LAZY_MAX_JUMP = 64.0
LAZY_MAX_SEED_KEYS = 256


def _attn_body(qt_ref, k_ref, vt_ref, o_ref, *, tk):
    tq = qt_ref.shape[3]
    s = k_ref.shape[2]
    qt = qt_ref[0, 0]
    nk = s // tk

    def scores(r0, rows):
        return jnp.dot(k_ref[0, 0, r0:r0 + rows, :], qt, preferred_element_type=F32)

    def pv(j, p):
        return jnp.dot(vt_ref[0, 0, :, j * tk:(j + 1) * tk], p.astype(BF16), preferred_element_type=F32)

    m = jnp.max(scores(0, min(LAZY_MAX_SEED_KEYS, tk)), axis=0, keepdims=True)
    l = jnp.zeros((1, tq), F32)
    acc = jnp.zeros((V_DIM, tq), F32)
    jump = jnp.zeros((1, tq), F32)
    for j in range(nk):
        st = scores(j * tk, tk)
        p = jnp.exp2(st - m)
        cmax = jnp.max(st, axis=0, keepdims=True)
        jump = jnp.maximum(jump, cmax - m)
        m_new = jnp.maximum(m, cmax)
        alpha = jnp.exp2(m - m_new)
        l = alpha * (l + jnp.sum(p, axis=0, keepdims=True))
        acc = alpha * (acc + pv(j, p))
        m = m_new
    o_ref[0] = (acc / l).T

    @pl.when(jnp.max(jump) > LAZY_MAX_JUMP)
    def _():
        m = jnp.full((1, tq), -jnp.inf, F32)
        l = jnp.zeros((1, tq), F32)
        acc = jnp.zeros((V_DIM, tq), F32)
        for j in range(nk):
            st = scores(j * tk, tk)
            m_new = jnp.maximum(m, jnp.max(st, axis=0, keepdims=True))
            p = jnp.exp2(st - m_new)
            alpha = jnp.exp2(m - m_new)
            l = alpha * l + jnp.sum(p, axis=0, keepdims=True)
            acc = alpha * acc + pv(j, p)
            m = m_new
        o_ref[0] = (acc / l).T


def _attention(qt, k, vt):
    b, nh, s, _ = k.shape
    tq = _pick(s, 1024)
    tk = _pick(s, 1024)
    return pl.pallas_call(
        functools.partial(_attn_body, tk=tk),
        grid=(b, nh, s // tq),
        in_specs=[
            pl.BlockSpec((1, 1, HEAD_PAD, tq), lambda bi, h, i: (bi, h, 0, i)),
            pl.BlockSpec((1, 1, s, HEAD_PAD), lambda bi, h, i: (bi, h, 0, 0)),
            pl.BlockSpec((1, 1, V_DIM, s), lambda bi, h, i: (bi, h, 0, 0)),
        ],
        out_specs=pl.BlockSpec((1, tq, V_DIM), lambda bi, h, i: (bi, i, h)),
        out_shape=jax.ShapeDtypeStruct((b, s, nh * V_DIM), F32),
        compiler_params=pltpu.CompilerParams(
            dimension_semantics=("parallel", "parallel", "parallel"),
            vmem_limit_bytes=V7X_VMEM_LIMIT_BYTES),
        name="attention",
    )(qt, k, vt)


def _mixer_out_body(x_ref, yr_ref, ya_ref, gr_ref, ga_ref, w_ref, o_ref):
    y = jnp.concatenate([_rms(yr_ref[...], gr_ref[...]).astype(BF16),
                         _rms(ya_ref[...], ga_ref[...]).astype(BF16)], axis=-1)
    o_ref[...] = x_ref[...] + jnp.dot(y, w_ref[...], preferred_element_type=F32)


def _mixer_out(x, yr, ya, gr, ga, w):
    t, d = x.shape
    tm = _pick(t, 512)
    return pl.pallas_call(
        _mixer_out_body,
        grid=(t // tm,),
        in_specs=[
            pl.BlockSpec((tm, d), lambda i: (i, 0)),
            pl.BlockSpec((tm, yr.shape[1]), lambda i: (i, 0)),
            pl.BlockSpec((tm, ya.shape[1]), lambda i: (i, 0)),
            _const_spec((1, yr.shape[1])),
            _const_spec((1, ya.shape[1])),
            _const_spec(w.shape),
        ],
        out_specs=pl.BlockSpec((tm, d), lambda i: (i, 0)),
        out_shape=jax.ShapeDtypeStruct((t, d), F32),
        compiler_params=pltpu.CompilerParams(
            dimension_semantics=("parallel",),
            vmem_limit_bytes=V7X_VMEM_LIMIT_BYTES),
        name="mixer_out",
    )(x, yr, ya, gr, ga, w)


def _rope_tables(seq):
    inv = 1.0 / (ROPE_THETA ** (jnp.arange(0, QK_ROPE, 2, dtype=F32) / QK_ROPE))
    ang = jnp.arange(seq, dtype=F32)[:, None] * inv[None, :]
    cos, sin = jnp.cos(ang), jnp.sin(ang)
    scale = math.log2(math.e) / math.sqrt(QK_NOPE + QK_ROPE)
    tqt = (scale * jnp.concatenate([jnp.ones((seq, QK_NOPE), F32), cos, cos, -sin, sin], axis=1)).T
    tkc = jnp.concatenate([cos, cos, cos, cos], axis=1)
    tks = jnp.concatenate([-sin, sin, -sin, sin], axis=1)
    return tqt, tkc, tks


def _layout_gate_up(w_gate, w_up):
    d, dff = w_gate.shape
    tf = _pick(dff, FFN_HIDDEN_TILE)
    w = jnp.stack([w_gate.reshape(d, dff // tf, tf), w_up.reshape(d, dff // tf, tf)], axis=2)
    return w.reshape(d, 2 * dff).astype(BF16)


def _layout_w_in(w_in):
    ko = 2 * RG_WIDTH + Q_LORA + KV_LORA
    k1 = w_in[:, ko:ko + ROPE_HALF]
    k2 = w_in[:, ko + ROPE_HALF:ko + QK_ROPE]
    return jnp.concatenate([w_in[:, :ko], k1, k2, k1, k2, k2, k1, k2, k1], axis=1).astype(BF16)


def _layout_w_q(w_q_b):
    w = w_q_b.reshape(Q_LORA, MLA_HEADS, QK_NOPE + QK_ROPE)
    x1 = w[:, :, QK_NOPE:QK_NOPE + ROPE_HALF]
    x2 = w[:, :, QK_NOPE + ROPE_HALF:]
    w = jnp.concatenate([w[:, :, :QK_NOPE], x1, x2, x2, x1], axis=-1)
    return w.reshape(Q_LORA, MLA_HEADS * HEAD_PAD).T.astype(BF16)


def _layout_w_kv(w_kv_b):
    w = w_kv_b.reshape(KV_LORA, MLA_HEADS, QK_NOPE + V_DIM)
    wk = w[:, :, :QK_NOPE].reshape(KV_LORA, -1)
    wvt = w[:, :, QK_NOPE:].reshape(KV_LORA, -1).T
    return wk.astype(BF16), wvt.astype(BF16)


def _layout_rg_gates(rg_w_a, rg_b_a, rg_w_x, rg_b_x, ct):
    g = ct // RG_BLOCK_W
    nt = RG_WIDTH // ct
    eye = jnp.eye(g, dtype=F32)

    def tiles(w):
        w = w.reshape(nt, g, RG_BLOCK_W, RG_BLOCK_W)
        return jnp.einsum("tgij,gk->tgikj", w, eye).reshape(nt, ct, ct)

    w = jnp.concatenate([tiles(rg_w_a[0]), tiles(rg_w_x[0]), tiles(rg_w_a[1]), tiles(rg_w_x[1])], axis=-1)
    bias = jnp.concatenate([rg_b_a[0].reshape(nt, 1, ct), rg_b_x[0].reshape(nt, 1, ct),
                            rg_b_a[1].reshape(nt, 1, ct), rg_b_x[1].reshape(nt, 1, ct)], axis=-1)
    return w.astype(BF16), bias


def _trunk(x, p):
    b, s, d = x.shape
    t = b * s
    row = lambda v: v.reshape(1, -1)
    x0 = x.reshape(t, d)
    x1 = _ffn(x0, row(p["ffn1_norm"]), p["ffn1_wgu"], p["ffn1_wd"], row(p["ffn1_norm"]), final_norm=False)

    tqt, tkc, tks = _rope_tables(s)
    xrg, grg, qt, k, vt = _mixer_in(x1.reshape(b, s, d), row(p["mix_norm"]), p["w_in"], row(p["q_a_norm"]),
                                    p["w_qt"], row(p["kv_a_norm"]), p["w_k"], p["w_vt"], tqt, tkc, tks)
    ct = 256 if s <= 4096 else 128
    gw, gb = p["rg_gates"][ct]
    y_rg = _rglru(xrg, grg, p["conv_w"], row(p["conv_b"]), gw, gb, p["rg_lambda"])
    y_att = _attention(qt, k, vt)
    x2 = _mixer_out(x1, y_rg.reshape(t, -1), y_att.reshape(t, -1), row(p["rg_out_norm"]),
                    row(p["attn_out_norm"]), p["w_out"])
    y = _ffn(x2, row(p["ffn2_norm"]), p["ffn2_wgu"], p["ffn2_wd"], row(p["final_norm"]), final_norm=True)
    return y.reshape(b, s, d)


def kernel(x_prompt, x_sample, ffn1_norm, ffn1_w_gate, ffn1_w_up, ffn1_w_down, mix_norm, w_in, conv_w, conv_b, rg_w_a, rg_b_a, rg_w_x, rg_b_x, rg_lambda, q_a_norm, w_q_b, kv_a_norm, w_kv_b, rg_out_norm, attn_out_norm, w_out, ffn2_norm, ffn2_w_gate, ffn2_w_up, ffn2_w_down, final_norm):
    depth = ffn1_norm.shape[0]

    def layer_params(l):
        w_k, w_vt = _layout_w_kv(w_kv_b[l])
        return {
            "ffn1_norm": ffn1_norm[l], "ffn1_wgu": _layout_gate_up(ffn1_w_gate[l], ffn1_w_up[l]),
            "ffn1_wd": ffn1_w_down[l].astype(BF16),
            "mix_norm": mix_norm[l], "w_in": _layout_w_in(w_in[l]),
            "conv_w": conv_w[l], "conv_b": conv_b[l],
            "rg_gates": {ct: _layout_rg_gates(rg_w_a[l], rg_b_a[l], rg_w_x[l], rg_b_x[l], ct)
                         for ct in (128, 256)},
            "rg_lambda": rg_lambda[l],
            "q_a_norm": q_a_norm[l], "w_qt": _layout_w_q(w_q_b[l]),
            "kv_a_norm": kv_a_norm[l], "w_k": w_k, "w_vt": w_vt,
            "rg_out_norm": rg_out_norm[l], "attn_out_norm": attn_out_norm[l],
            "w_out": w_out[l].astype(BF16),
            "ffn2_norm": ffn2_norm[l], "ffn2_wgu": _layout_gate_up(ffn2_w_gate[l], ffn2_w_up[l]),
            "ffn2_wd": ffn2_w_down[l].astype(BF16),
            "final_norm": final_norm,
        }

    assert depth == 1, "the final rmsnorm is fused into the last layer's second FFN"
    p = layer_params(0)
    return (_trunk(x_prompt, p), _trunk(x_sample, p))
```

```python
import functools
import math

import jax
import jax.numpy as jnp
from jax import lax
from jax.experimental import pallas as pl
from jax.experimental.pallas import tpu as pltpu

F32 = jnp.float32
BF16 = jnp.bfloat16

NORM_EPS = 1e-6
RG_WIDTH = 1024
RG_BLOCKS = 16
RG_BLOCK_W = RG_WIDTH // RG_BLOCKS
RG_C = 8.0
CONV_W = 4
CONV_PAD_L = 2
MLA_HEADS = 8
Q_LORA = 512
KV_LORA = 256
QK_NOPE = 128
QK_ROPE = 64
V_DIM = 128
ROPE_THETA = 10000.0
ROPE_HALF = QK_ROPE // 2
HEAD_PAD = 256

V7X_SUBLANES = 8
V7X_LANES = 128
V7X_VMEM_LIMIT_BYTES = 60000 * 1024


def _pick(dim, pref):
    t = min(dim, pref)
    while dim % t:
        t //= 2
    return t


def _rms(x, g):
    return x * lax.rsqrt(jnp.mean(x * x, axis=-1, keepdims=True) + NORM_EPS) * g


def _sigmoid(x):
    return 0.5 * jnp.tanh(0.5 * x) + 0.5


def _const_spec(shape):
    nd = len(shape)
    return pl.BlockSpec(shape, lambda *_: (0,) * nd, pipeline_mode=pl.Buffered(1))


def _ffn_body(x_ref, g_ref, wg_ref, wu_ref, wd_ref, pg_ref, o_ref, xn_ref, *, final_norm):
    f = pl.program_id(1)

    def half_swiglu(xn):
        gate = jnp.dot(xn, wg_ref[...], preferred_element_type=F32)
        up = jnp.dot(xn, wu_ref[...], preferred_element_type=F32)
        h = (gate * jax.nn.sigmoid(gate)) * (0.5 * up)
        return jnp.dot(h.astype(BF16), wd_ref[...], preferred_element_type=F32)

    @pl.when(f == 0)
    def _():
        x = x_ref[...]
        xn = _rms(x, g_ref[...]).astype(BF16)
        xn_ref[...] = xn
        o_ref[...] = x + half_swiglu(xn)

    @pl.when(f > 0)
    def _():
        o_ref[...] += half_swiglu(xn_ref[...])

    if final_norm:
        @pl.when(f == pl.num_programs(1) - 1)
        def _():
            o_ref[...] = _rms(o_ref[...], pg_ref[...])


def _ffn(x, g, wg, wu, wd, pg, *, final_norm):
    t, d = x.shape
    dff = wg.shape[1]
    tm = _pick(t, 1024)
    tf = _pick(dff, 512)
    return pl.pallas_call(
        functools.partial(_ffn_body, final_norm=final_norm),
        grid=(t // tm, dff // tf),
        in_specs=[
            pl.BlockSpec((tm, d), lambda i, f: (i, 0)),
            _const_spec((1, d)),
            pl.BlockSpec((d, tf), lambda i, f: (0, f)),
            pl.BlockSpec((d, tf), lambda i, f: (0, f)),
            pl.BlockSpec((tf, d), lambda i, f: (f, 0)),
            _const_spec((1, d)),
        ],
        out_specs=pl.BlockSpec((tm, d), lambda i, f: (i, 0)),
        out_shape=jax.ShapeDtypeStruct((t, d), F32),
        scratch_shapes=[pltpu.VMEM((tm, d), BF16)],
        compiler_params=pltpu.CompilerParams(
            dimension_semantics=("parallel", "arbitrary"),
            vmem_limit_bytes=V7X_VMEM_LIMIT_BYTES),
        name="ffn",
    )(x, g, wg, wu, wd, pg)


def _mixer_in_body(x_ref, g_ref, win_ref, qg_ref, wqt_ref, kvg_ref, wk_ref, wvt_ref, tqt_ref, tkc_ref, tks_ref,
                   xrg_ref, grg_ref, qt_ref, k_ref, vt_ref):
    hn = _rms(x_ref[0], g_ref[...]).astype(BF16)
    z = jnp.dot(hn, win_ref[...], preferred_element_type=F32)
    o = 0
    xrg_ref[0] = z[:, o:o + RG_WIDTH]
    o += RG_WIDTH
    grg_ref[0] = z[:, o:o + RG_WIDTH]
    o += RG_WIDTH
    qc = z[:, o:o + Q_LORA]
    o += Q_LORA
    kvc = z[:, o:o + KV_LORA]
    o += KV_LORA
    ka = z[:, o:o + V7X_LANES]
    kb = z[:, o + V7X_LANES:o + 2 * V7X_LANES]

    nt = (((1,), (1,)), ((), ()))
    qn = _rms(qc, qg_ref[...]).astype(BF16)
    kvn = _rms(kvc, kvg_ref[...]).astype(BF16)
    qt = lax.dot_general(wqt_ref[...], qn, nt, preferred_element_type=F32)
    vt = lax.dot_general(wvt_ref[...], kvn, nt, preferred_element_type=F32)
    kn = jnp.dot(kvn, wk_ref[...], preferred_element_type=F32)
    kr = (ka * tkc_ref[...] + kb * tks_ref[...]).astype(BF16)
    tqt = tqt_ref[...]
    for h in range(MLA_HEADS):
        qt_ref[0, h] = (qt[h * HEAD_PAD:(h + 1) * HEAD_PAD, :] * tqt).astype(BF16)
        k_ref[0, h, :, 0:QK_NOPE] = kn[:, h * QK_NOPE:(h + 1) * QK_NOPE].astype(BF16)
        k_ref[0, h, :, QK_NOPE:HEAD_PAD] = kr
        vt_ref[0, h] = vt[h * V_DIM:(h + 1) * V_DIM, :].astype(BF16)


def _mixer_in(x, g, win, qg, wqt, kvg, wk, wvt, tqt, tkc, tks):
    b, s, d = x.shape
    tm = _pick(s, 512)
    din = win.shape[1]
    nh = MLA_HEADS
    return pl.pallas_call(
        _mixer_in_body,
        grid=(b, s // tm),
        in_specs=[
            pl.BlockSpec((1, tm, d), lambda bi, i: (bi, i, 0)),
            _const_spec((1, d)),
            _const_spec((d, din)),
            _const_spec((1, Q_LORA)),
            _const_spec((nh * HEAD_PAD, Q_LORA)),
            _const_spec((1, KV_LORA)),
            _const_spec((KV_LORA, nh * QK_NOPE)),
            _const_spec((nh * V_DIM, KV_LORA)),
            pl.BlockSpec((HEAD_PAD, tm), lambda bi, i: (0, i)),
            pl.BlockSpec((tm, V7X_LANES), lambda bi, i: (i, 0)),
            pl.BlockSpec((tm, V7X_LANES), lambda bi, i: (i, 0)),
        ],
        out_specs=[
            pl.BlockSpec((1, tm, RG_WIDTH), lambda bi, i: (bi, i, 0)),
            pl.BlockSpec((1, tm, RG_WIDTH), lambda bi, i: (bi, i, 0)),
            pl.BlockSpec((1, nh, HEAD_PAD, tm), lambda bi, i: (bi, 0, 0, i)),
            pl.BlockSpec((1, nh, tm, HEAD_PAD), lambda bi, i: (bi, 0, i, 0)),
            pl.BlockSpec((1, nh, V_DIM, tm), lambda bi, i: (bi, 0, 0, i)),
        ],
        out_shape=[
            jax.ShapeDtypeStruct((b, s, RG_WIDTH), F32),
            jax.ShapeDtypeStruct((b, s, RG_WIDTH), F32),
            jax.ShapeDtypeStruct((b, nh, HEAD_PAD, s), BF16),
            jax.ShapeDtypeStruct((b, nh, s, HEAD_PAD), BF16),
            jax.ShapeDtypeStruct((b, nh, V_DIM, s), BF16),
        ],
        compiler_params=pltpu.CompilerParams(
            dimension_semantics=("parallel", "parallel"),
            vmem_limit_bytes=V7X_VMEM_LIMIT_BYTES),
        name="mixer_in",
    )(x, g, win, qg, wqt, kvg, wk, wvt, tqt, tkc, tks)


RG_SEGMENTS = V7X_SUBLANES
RG_SEG_PAD = V7X_SUBLANES


def _rglru_body(x_ref, g_ref, cw_ref, cb_ref, w_ref, bias_ref, lam_ref, y_ref,
                xpad_ref, af_ref, bf_ref, ab_ref, bb_ref, cf_ref, cr_ref, *, tc):
    s, ct = x_ref.shape[1], x_ref.shape[2]
    n = V7X_SUBLANES
    nl = V7X_LANES
    nslab = ct // nl
    seg_len = s // RG_SEGMENTS
    pitch = seg_len + RG_SEG_PAD
    nchunks = s // tc
    halo = n

    zeros = jnp.zeros((halo, ct), F32)
    xpad_ref[0:halo, :] = zeros
    xpad_ref[halo + s:halo + s + halo, :] = zeros
    xpad_ref[halo:halo + s, :] = x_ref[0]

    cw = cw_ref[...]
    cb = cb_ref[...]
    decay = (-RG_C * math.log2(math.e)) * jax.nn.softplus(-lam_ref[...])
    bias = bias_ref[0]

    def seg_row(t0):
        seg = t0 // seg_len
        return seg, pl.multiple_of(seg * pitch + (t0 - seg * seg_len), n)

    def phase1(c, _):
        t0 = pl.multiple_of(c * tc, tc)
        blk = xpad_ref[pl.ds(t0, tc + 2 * halo), :]
        xc = cb
        for k in range(CONV_W):
            o = halo - CONV_PAD_L + k
            xc = xc + cw[k:k + 1, :] * blk[o:o + tc, :]
        gates = jnp.dot(xc.astype(BF16), w_ref[0], preferred_element_type=F32) + bias
        _, r0 = seg_row(t0)
        for d, (a_ref, b_ref) in enumerate(((af_ref, bf_ref), (ab_ref, bb_ref))):
            r = _sigmoid(gates[:, 2 * d * ct:(2 * d + 1) * ct])
            i = _sigmoid(gates[:, (2 * d + 1) * ct:(2 * d + 2) * ct])
            a = jnp.exp2(r * decay[d:d + 1, :])
            y = 1.0 - a * a
            bterm = (y * lax.rsqrt(jnp.maximum(y, 1e-30))) * (i * xc)
            for sl in range(nslab):
                a_ref[sl, pl.ds(r0, tc), :] = a[:, sl * nl:(sl + 1) * nl]
                b_ref[sl, pl.ds(r0, tc), :] = bterm[:, sl * nl:(sl + 1) * nl]
        return 0

    lax.fori_loop(0, nchunks, phase1, 0)

    unroll = n // nslab

    def scan_group(tls, a_ref, b_ref, state):
        rows = [pl.ds(tl, RG_SEGMENTS, stride=pitch) for tl in tls]
        vals = [[(a_ref[sl, r, :], b_ref[sl, r, :]) for r in rows] for sl in range(nslab)]
        outs = []
        for sl in range(nslab):
            acum, h = state[sl]
            res = []
            for a, b in vals[sl]:
                acum = a * acum
                h = a * h + b
                res.append((acum, h))
            state[sl] = (acum, h)
            outs.append(res)
        for sl in range(nslab):
            for r, (acum, h) in zip(rows, outs[sl]):
                a_ref[sl, r, :] = acum
                b_ref[sl, r, :] = h

    ones = jnp.ones((RG_SEGMENTS, nl), F32)
    zero = jnp.zeros((RG_SEGMENTS, nl), F32)

    def phase2(it, carry):
        fwd = [carry[2 * sl] for sl in range(nslab)]
        rev = [carry[2 * sl + 1] for sl in range(nslab)]
        t0 = it * unroll
        scan_group([t0 + u for u in range(unroll)], af_ref, bf_ref, fwd)
        scan_group([seg_len - 1 - t0 - u for u in range(unroll)], ab_ref, bb_ref, rev)
        return tuple(x for sl in range(nslab) for x in (fwd[sl], rev[sl]))

    ends = lax.fori_loop(0, seg_len // unroll, phase2, tuple((ones, zero) for _ in range(2 * nslab)))

    seg_id = lax.broadcasted_iota(jnp.int32, (RG_SEGMENTS, nl), 0)
    for sl in range(nslab):
        for d, c_ref in enumerate((cf_ref, cr_ref)):
            acum, h = ends[2 * sl + d]
            carry = zero
            for k in range(1, RG_SEGMENTS):
                full = h + acum * carry
                if d == 0:
                    carry = jnp.where(seg_id == k, pltpu.roll(full, 1, 0), carry)
                else:
                    kk = RG_SEGMENTS - 1 - k
                    carry = jnp.where(seg_id == kk, pltpu.roll(full, RG_SEGMENTS - 1, 0), carry)
            c_ref[:, sl * nl:(sl + 1) * nl] = carry

    def phase3(c, _):
        t0 = pl.multiple_of(c * tc, tc)
        seg, r0 = seg_row(t0)
        cf = cf_ref[pl.ds(seg, 1), :]
        cr = cr_ref[pl.ds(seg, 1), :]
        for sl in range(nslab):
            lanes = slice(sl * nl, (sl + 1) * nl)
            hf = bf_ref[sl, pl.ds(r0, tc), :] + af_ref[sl, pl.ds(r0, tc), :] * cf[:, lanes]
            hr = bb_ref[sl, pl.ds(r0, tc), :] + ab_ref[sl, pl.ds(r0, tc), :] * cr[:, lanes]
            y_ref[0, pl.ds(t0, tc), lanes] = (hf + hr) * jax.nn.gelu(g_ref[0, pl.ds(t0, tc), lanes])
        return 0

    lax.fori_loop(0, nchunks, phase3, 0)


def _rglru(xrg, grg, cw, cb, w, bias, lam):
    b, s, c = xrg.shape
    ct = w.shape[1]
    seg_len = s // RG_SEGMENTS
    tc = _pick(seg_len, 256)
    seg_rows = RG_SEGMENTS * (seg_len + RG_SEG_PAD)
    seg_scratch = pltpu.VMEM((ct // V7X_LANES, seg_rows, V7X_LANES), F32)
    return pl.pallas_call(
        functools.partial(_rglru_body, tc=tc),
        grid=(b, c // ct),
        in_specs=[
            pl.BlockSpec((1, s, ct), lambda bi, j: (bi, 0, j)),
            pl.BlockSpec((1, s, ct), lambda bi, j: (bi, 0, j)),
            pl.BlockSpec((CONV_W, ct), lambda bi, j: (0, j)),
            pl.BlockSpec((1, ct), lambda bi, j: (0, j)),
            pl.BlockSpec((1, ct, 4 * ct), lambda bi, j: (j, 0, 0)),
            pl.BlockSpec((1, 1, 4 * ct), lambda bi, j: (j, 0, 0)),
            pl.BlockSpec((2, ct), lambda bi, j: (0, j)),
        ],
        out_specs=pl.BlockSpec((1, s, ct), lambda bi, j: (bi, 0, j)),
        out_shape=jax.ShapeDtypeStruct((b, s, c), F32),
        scratch_shapes=[pltpu.VMEM((s + 2 * V7X_SUBLANES, ct), F32),
                        seg_scratch, seg_scratch, seg_scratch, seg_scratch,
                        pltpu.VMEM((RG_SEGMENTS, ct), F32), pltpu.VMEM((RG_SEGMENTS, ct), F32)],
        compiler_params=pltpu.CompilerParams(
            dimension_semantics=("parallel", "parallel"),
            vmem_limit_bytes=V7X_VMEM_LIMIT_BYTES),
        name="rglru",
    )(xrg, grg, cw, cb, w, bias, lam)


LAZY_MAX_JUMP = 64.0
LAZY_MAX_SEED_KEYS = 256


def _attn_body(qt_ref, k_ref, vt_ref, o_ref, *, tk):
    tq = qt_ref.shape[3]
    s = k_ref.shape[2]
    qt = qt_ref[0, 0]
    nk = s // tk

    def scores(r0, rows):
        return jnp.dot(k_ref[0, 0, r0:r0 + rows, :], qt, preferred_element_type=F32)

    def pv(j, p):
        return jnp.dot(vt_ref[0, 0, :, j * tk:(j + 1) * tk], p.astype(BF16), preferred_element_type=F32)

    m = jnp.max(scores(0, min(LAZY_MAX_SEED_KEYS, tk)), axis=0, keepdims=True)
    l = jnp.zeros((1, tq), F32)
    acc = jnp.zeros((V_DIM, tq), F32)
    jump = jnp.zeros((1, tq), F32)
    for j in range(nk):
        st = scores(j * tk, tk)
        p = jnp.exp2(st - m)
        cmax = jnp.max(st, axis=0, keepdims=True)
        jump = jnp.maximum(jump, cmax - m)
        m_new = jnp.maximum(m, cmax)
        alpha = jnp.exp2(m - m_new)
        l = alpha * (l + jnp.sum(p, axis=0, keepdims=True))
        acc = alpha * (acc + pv(j, p))
        m = m_new
    o_ref[0] = (acc / l).T

    @pl.when(jnp.max(jump) > LAZY_MAX_JUMP)
    def _():
        m = jnp.full((1, tq), -jnp.inf, F32)
        l = jnp.zeros((1, tq), F32)
        acc = jnp.zeros((V_DIM, tq), F32)
        for j in range(nk):
            st = scores(j * tk, tk)
            m_new = jnp.maximum(m, jnp.max(st, axis=0, keepdims=True))
            p = jnp.exp2(st - m_new)
            alpha = jnp.exp2(m - m_new)
            l = alpha * l + jnp.sum(p, axis=0, keepdims=True)
            acc = alpha * acc + pv(j, p)
            m = m_new
        o_ref[0] = (acc / l).T


def _attention(qt, k, vt):
    b, nh, s, _ = k.shape
    tq = _pick(s, 1024)
    tk = _pick(max(s // 2, V7X_LANES), 2048)
    return pl.pallas_call(
        functools.partial(_attn_body, tk=tk),
        grid=(b, nh, s // tq),
        in_specs=[
            pl.BlockSpec((1, 1, HEAD_PAD, tq), lambda bi, h, i: (bi, h, 0, i)),
            pl.BlockSpec((1, 1, s, HEAD_PAD), lambda bi, h, i: (bi, h, 0, 0)),
            pl.BlockSpec((1, 1, V_DIM, s), lambda bi, h, i: (bi, h, 0, 0)),
        ],
        out_specs=pl.BlockSpec((1, tq, V_DIM), lambda bi, h, i: (bi, i, h)),
        out_shape=jax.ShapeDtypeStruct((b, s, nh * V_DIM), F32),
        compiler_params=pltpu.CompilerParams(
            dimension_semantics=("parallel", "parallel", "parallel"),
            vmem_limit_bytes=V7X_VMEM_LIMIT_BYTES),
        name="attention",
    )(qt, k, vt)


def _mixer_out_body(x_ref, yr_ref, ya_ref, gr_ref, ga_ref, w_ref, o_ref):
    y = jnp.concatenate([_rms(yr_ref[...], gr_ref[...]).astype(BF16),
                         _rms(ya_ref[...], ga_ref[...]).astype(BF16)], axis=-1)
    o_ref[...] = x_ref[...] + jnp.dot(y, w_ref[...], preferred_element_type=F32)


def _mixer_out(x, yr, ya, gr, ga, w):
    t, d = x.shape
    tm = _pick(t, 512)
    return pl.pallas_call(
        _mixer_out_body,
        grid=(t // tm,),
        in_specs=[
            pl.BlockSpec((tm, d), lambda i: (i, 0)),
            pl.BlockSpec((tm, yr.shape[1]), lambda i: (i, 0)),
            pl.BlockSpec((tm, ya.shape[1]), lambda i: (i, 0)),
            _const_spec((1, yr.shape[1])),
            _const_spec((1, ya.shape[1])),
            _const_spec(w.shape),
        ],
        out_specs=pl.BlockSpec((tm, d), lambda i: (i, 0)),
        out_shape=jax.ShapeDtypeStruct((t, d), F32),
        compiler_params=pltpu.CompilerParams(
            dimension_semantics=("parallel",),
            vmem_limit_bytes=V7X_VMEM_LIMIT_BYTES),
        name="mixer_out",
    )(x, yr, ya, gr, ga, w)


def _rope_tables(seq):
    inv = 1.0 / (ROPE_THETA ** (jnp.arange(0, QK_ROPE, 2, dtype=F32) / QK_ROPE))
    ang = jnp.arange(seq, dtype=F32)[:, None] * inv[None, :]
    cos, sin = jnp.cos(ang), jnp.sin(ang)
    scale = math.log2(math.e) / math.sqrt(QK_NOPE + QK_ROPE)
    tqt = (scale * jnp.concatenate([jnp.ones((seq, QK_NOPE), F32), cos, cos, -sin, sin], axis=1)).T
    tkc = jnp.concatenate([cos, cos, cos, cos], axis=1)
    tks = jnp.concatenate([-sin, sin, -sin, sin], axis=1)
    return tqt, tkc, tks


def _layout_w_in(w_in):
    ko = 2 * RG_WIDTH + Q_LORA + KV_LORA
    k1 = w_in[:, ko:ko + ROPE_HALF]
    k2 = w_in[:, ko + ROPE_HALF:ko + QK_ROPE]
    return jnp.concatenate([w_in[:, :ko], k1, k2, k1, k2, k2, k1, k2, k1], axis=1).astype(BF16)


def _layout_w_q(w_q_b):
    w = w_q_b.reshape(Q_LORA, MLA_HEADS, QK_NOPE + QK_ROPE)
    x1 = w[:, :, QK_NOPE:QK_NOPE + ROPE_HALF]
    x2 = w[:, :, QK_NOPE + ROPE_HALF:]
    w = jnp.concatenate([w[:, :, :QK_NOPE], x1, x2, x2, x1], axis=-1)
    return w.reshape(Q_LORA, MLA_HEADS * HEAD_PAD).T.astype(BF16)


def _layout_w_kv(w_kv_b):
    w = w_kv_b.reshape(KV_LORA, MLA_HEADS, QK_NOPE + V_DIM)
    wk = w[:, :, :QK_NOPE].reshape(KV_LORA, -1)
    wvt = w[:, :, QK_NOPE:].reshape(KV_LORA, -1).T
    return wk.astype(BF16), wvt.astype(BF16)


def _layout_rg_gates(rg_w_a, rg_b_a, rg_w_x, rg_b_x, ct):
    g = ct // RG_BLOCK_W
    nt = RG_WIDTH // ct
    eye = jnp.eye(g, dtype=F32)

    def tiles(w):
        w = w.reshape(nt, g, RG_BLOCK_W, RG_BLOCK_W)
        return jnp.einsum("tgij,gk->tgikj", w, eye).reshape(nt, ct, ct)

    w = jnp.concatenate([tiles(rg_w_a[0]), tiles(rg_w_x[0]), tiles(rg_w_a[1]), tiles(rg_w_x[1])], axis=-1)
    bias = jnp.concatenate([rg_b_a[0].reshape(nt, 1, ct), rg_b_x[0].reshape(nt, 1, ct),
                            rg_b_a[1].reshape(nt, 1, ct), rg_b_x[1].reshape(nt, 1, ct)], axis=-1)
    return w.astype(BF16), bias


def _trunk(x, p):
    b, s, d = x.shape
    t = b * s
    row = lambda v: v.reshape(1, -1)
    x0 = x.reshape(t, d)
    x1 = _ffn(x0, row(p["ffn1_norm"]), p["ffn1_wg"], p["ffn1_wu"], p["ffn1_wd"], row(p["ffn1_norm"]),
              final_norm=False)

    tqt, tkc, tks = _rope_tables(s)
    xrg, grg, qt, k, vt = _mixer_in(x1.reshape(b, s, d), row(p["mix_norm"]), p["w_in"], row(p["q_a_norm"]),
                                    p["w_qt"], row(p["kv_a_norm"]), p["w_k"], p["w_vt"], tqt, tkc, tks)
    ct = 256 if s <= 4096 else 128
    gw, gb = p["rg_gates"][ct]
    y_rg = _rglru(xrg, grg, p["conv_w"], row(p["conv_b"]), gw, gb, p["rg_lambda"])
    y_att = _attention(qt, k, vt)
    x2 = _mixer_out(x1, y_rg.reshape(t, -1), y_att.reshape(t, -1), row(p["rg_out_norm"]),
                    row(p["attn_out_norm"]), p["w_out"])
    y = _ffn(x2, row(p["ffn2_norm"]), p["ffn2_wg"], p["ffn2_wu"], p["ffn2_wd"], row(p["final_norm"]),
             final_norm=True)
    return y.reshape(b, s, d)


def kernel(x_prompt, x_sample, ffn1_norm, ffn1_w_gate, ffn1_w_up, ffn1_w_down, mix_norm, w_in, conv_w, conv_b, rg_w_a, rg_b_a, rg_w_x, rg_b_x, rg_lambda, q_a_norm, w_q_b, kv_a_norm, w_kv_b, rg_out_norm, attn_out_norm, w_out, ffn2_norm, ffn2_w_gate, ffn2_w_up, ffn2_w_down, final_norm):
    depth = ffn1_norm.shape[0]

    def layer_params(l):
        w_k, w_vt = _layout_w_kv(w_kv_b[l])
        return {
            "ffn1_norm": ffn1_norm[l], "ffn1_wg": ffn1_w_gate[l].astype(BF16),
            "ffn1_wu": ffn1_w_up[l].astype(BF16), "ffn1_wd": ffn1_w_down[l].astype(BF16),
            "mix_norm": mix_norm[l], "w_in": _layout_w_in(w_in[l]),
            "conv_w": conv_w[l], "conv_b": conv_b[l],
            "rg_gates": {ct: _layout_rg_gates(rg_w_a[l], rg_b_a[l], rg_w_x[l], rg_b_x[l], ct)
                         for ct in (128, 256)},
            "rg_lambda": rg_lambda[l],
            "q_a_norm": q_a_norm[l], "w_qt": _layout_w_q(w_q_b[l]),
            "kv_a_norm": kv_a_norm[l], "w_k": w_k, "w_vt": w_vt,
            "rg_out_norm": rg_out_norm[l], "attn_out_norm": attn_out_norm[l],
            "w_out": w_out[l].astype(BF16),
            "ffn2_norm": ffn2_norm[l], "ffn2_wg": ffn2_w_gate[l].astype(BF16),
            "ffn2_wu": ffn2_w_up[l].astype(BF16), "ffn2_wd": ffn2_w_down[l].astype(BF16),
            "final_norm": final_norm,
        }

    assert depth == 1, "the final rmsnorm is fused into the last layer's second FFN"
    p = layer_params(0)
    return (_trunk(x_prompt, p), _trunk(x_sample, p))
```

```python
import functools
import math

import jax
import jax.numpy as jnp
from jax import lax
from jax.experimental import pallas as pl
from jax.experimental.pallas import tpu as pltpu

F32 = jnp.float32
BF16 = jnp.bfloat16

NORM_EPS = 1e-6
RG_WIDTH = 1024
RG_BLOCKS = 16
RG_BLOCK_W = RG_WIDTH // RG_BLOCKS
RG_C = 8.0
CONV_W = 4
CONV_PAD_L = 2
MLA_HEADS = 8
Q_LORA = 512
KV_LORA = 256
QK_NOPE = 128
QK_ROPE = 64
V_DIM = 128
ROPE_THETA = 10000.0
ROPE_HALF = QK_ROPE // 2
HEAD_PAD = 256

V7X_SUBLANES = 8
V7X_LANES = 128
V7X_VMEM_LIMIT_BYTES = 60000 * 1024


def _pick(dim, pref):
    t = min(dim, pref)
    while dim % t:
        t //= 2
    return t


def _rms(x, g):
    return x * lax.rsqrt(jnp.mean(x * x, axis=-1, keepdims=True) + NORM_EPS) * g


def _const_spec(shape):
    nd = len(shape)
    return pl.BlockSpec(shape, lambda *_: (0,) * nd, pipeline_mode=pl.Buffered(1))


def _ffn_body(x_ref, g_ref, wg_ref, wu_ref, wd_ref, pg_ref, o_ref, xn_ref, *, final_norm):
    f = pl.program_id(1)

    def half_swiglu(xn):
        gate = jnp.dot(xn, wg_ref[...], preferred_element_type=F32)
        up = jnp.dot(xn, wu_ref[...], preferred_element_type=F32)
        h = (gate * jax.nn.sigmoid(gate)) * (0.5 * up)
        return jnp.dot(h.astype(BF16), wd_ref[...], preferred_element_type=F32)

    @pl.when(f == 0)
    def _():
        x = x_ref[...]
        xn = _rms(x, g_ref[...]).astype(BF16)
        xn_ref[...] = xn
        o_ref[...] = x + half_swiglu(xn)

    @pl.when(f > 0)
    def _():
        o_ref[...] += half_swiglu(xn_ref[...])

    if final_norm:
        @pl.when(f == pl.num_programs(1) - 1)
        def _():
            o_ref[...] = _rms(o_ref[...], pg_ref[...])


def _ffn(x, g, wg, wu, wd, pg, *, final_norm):
    t, d = x.shape
    dff = wg.shape[1]
    tm = _pick(t, 1024)
    tf = _pick(dff, 512)
    return pl.pallas_call(
        functools.partial(_ffn_body, final_norm=final_norm),
        grid=(t // tm, dff // tf),
        in_specs=[
            pl.BlockSpec((tm, d), lambda i, f: (i, 0)),
            _const_spec((1, d)),
            pl.BlockSpec((d, tf), lambda i, f: (0, f)),
            pl.BlockSpec((d, tf), lambda i, f: (0, f)),
            pl.BlockSpec((tf, d), lambda i, f: (f, 0)),
            _const_spec((1, d)),
        ],
        out_specs=pl.BlockSpec((tm, d), lambda i, f: (i, 0)),
        out_shape=jax.ShapeDtypeStruct((t, d), F32),
        scratch_shapes=[pltpu.VMEM((tm, d), BF16)],
        compiler_params=pltpu.CompilerParams(
            dimension_semantics=("parallel", "arbitrary"),
            vmem_limit_bytes=V7X_VMEM_LIMIT_BYTES),
        name="ffn",
    )(x, g, wg, wu, wd, pg)


def _mixer_in_body(x_ref, g_ref, win_ref, qg_ref, wqt_ref, kvg_ref, wk_ref, wvt_ref, tqt_ref, tkc_ref, tks_ref,
                   xrg_ref, grg_ref, qt_ref, k_ref, vt_ref):
    hn = _rms(x_ref[0], g_ref[...]).astype(BF16)
    z = jnp.dot(hn, win_ref[...], preferred_element_type=F32)
    o = 0
    xrg_ref[0] = z[:, o:o + RG_WIDTH]
    o += RG_WIDTH
    grg_ref[0] = z[:, o:o + RG_WIDTH]
    o += RG_WIDTH
    qc = z[:, o:o + Q_LORA]
    o += Q_LORA
    kvc = z[:, o:o + KV_LORA]
    o += KV_LORA
    ka = z[:, o:o + V7X_LANES]
    kb = z[:, o + V7X_LANES:o + 2 * V7X_LANES]

    nt = (((1,), (1,)), ((), ()))
    qn = _rms(qc, qg_ref[...]).astype(BF16)
    kvn = _rms(kvc, kvg_ref[...]).astype(BF16)
    qt = lax.dot_general(wqt_ref[...], qn, nt, preferred_element_type=F32)
    vt = lax.dot_general(wvt_ref[...], kvn, nt, preferred_element_type=F32)
    kn = jnp.dot(kvn, wk_ref[...], preferred_element_type=F32)
    kr = (ka * tkc_ref[...] + kb * tks_ref[...]).astype(BF16)
    tqt = tqt_ref[...]
    for h in range(MLA_HEADS):
        qt_ref[0, h] = (qt[h * HEAD_PAD:(h + 1) * HEAD_PAD, :] * tqt).astype(BF16)
        k_ref[0, h, :, 0:QK_NOPE] = kn[:, h * QK_NOPE:(h + 1) * QK_NOPE].astype(BF16)
        k_ref[0, h, :, QK_NOPE:HEAD_PAD] = kr
        vt_ref[0, h] = vt[h * V_DIM:(h + 1) * V_DIM, :].astype(BF16)


def _mixer_in(x, g, win, qg, wqt, kvg, wk, wvt, tqt, tkc, tks):
    b, s, d = x.shape
    tm = _pick(s, 512)
    din = win.shape[1]
    nh = MLA_HEADS
    return pl.pallas_call(
        _mixer_in_body,
        grid=(b, s // tm),
        in_specs=[
            pl.BlockSpec((1, tm, d), lambda bi, i: (bi, i, 0)),
            _const_spec((1, d)),
            _const_spec((d, din)),
            _const_spec((1, Q_LORA)),
            _const_spec((nh * HEAD_PAD, Q_LORA)),
            _const_spec((1, KV_LORA)),
            _const_spec((KV_LORA, nh * QK_NOPE)),
            _const_spec((nh * V_DIM, KV_LORA)),
            pl.BlockSpec((HEAD_PAD, tm), lambda bi, i: (0, i)),
            pl.BlockSpec((tm, V7X_LANES), lambda bi, i: (i, 0)),
            pl.BlockSpec((tm, V7X_LANES), lambda bi, i: (i, 0)),
        ],
        out_specs=[
            pl.BlockSpec((1, tm, RG_WIDTH), lambda bi, i: (bi, i, 0)),
            pl.BlockSpec((1, tm, RG_WIDTH), lambda bi, i: (bi, i, 0)),
            pl.BlockSpec((1, nh, HEAD_PAD, tm), lambda bi, i: (bi, 0, 0, i)),
            pl.BlockSpec((1, nh, tm, HEAD_PAD), lambda bi, i: (bi, 0, i, 0)),
            pl.BlockSpec((1, nh, V_DIM, tm), lambda bi, i: (bi, 0, 0, i)),
        ],
        out_shape=[
            jax.ShapeDtypeStruct((b, s, RG_WIDTH), F32),
            jax.ShapeDtypeStruct((b, s, RG_WIDTH), F32),
            jax.ShapeDtypeStruct((b, nh, HEAD_PAD, s), BF16),
            jax.ShapeDtypeStruct((b, nh, s, HEAD_PAD), BF16),
            jax.ShapeDtypeStruct((b, nh, V_DIM, s), BF16),
        ],
        compiler_params=pltpu.CompilerParams(
            dimension_semantics=("parallel", "parallel"),
            vmem_limit_bytes=V7X_VMEM_LIMIT_BYTES),
        name="mixer_in",
    )(x, g, win, qg, wqt, kvg, wk, wvt, tqt, tkc, tks)


RG_SEGMENTS = V7X_SUBLANES
RG_SEG_PAD = V7X_SUBLANES


def _rglru_body(x_ref, g_ref, cw_ref, cb_ref, w_ref, bias_ref, lam_ref, y_ref,
                xpad_ref, af_ref, bf_ref, ab_ref, bb_ref, cf_ref, cr_ref, *, tc):
    s, ct = x_ref.shape[1], x_ref.shape[2]
    n = V7X_SUBLANES
    nl = V7X_LANES
    nslab = ct // nl
    seg_len = s // RG_SEGMENTS
    pitch = seg_len + RG_SEG_PAD
    nchunks = s // tc
    halo = n

    zeros = jnp.zeros((halo, ct), F32)
    xpad_ref[0:halo, :] = zeros
    xpad_ref[halo + s:halo + s + halo, :] = zeros
    xpad_ref[halo:halo + s, :] = x_ref[0]

    cw = cw_ref[...]
    cb = cb_ref[...]
    half_decay = (-0.5 * RG_C * math.log2(math.e)) * jax.nn.softplus(-lam_ref[...])
    bias = bias_ref[0]

    def seg_row(t0):
        seg = t0 // seg_len
        return seg, pl.multiple_of(seg * pitch + (t0 - seg * seg_len), n)

    def phase1(c, _):
        t0 = pl.multiple_of(c * tc, tc)
        blk = xpad_ref[pl.ds(t0, tc + 2 * halo), :]
        xc = cb
        for k in range(CONV_W):
            o = halo - CONV_PAD_L + k
            xc = xc + cw[k:k + 1, :] * blk[o:o + tc, :]
        half_g = jnp.dot(xc.astype(BF16), w_ref[0], preferred_element_type=F32) + bias
        half_xc = 0.5 * xc
        _, r0 = seg_row(t0)
        for d, (a_ref, b_ref) in enumerate(((af_ref, bf_ref), (ab_ref, bb_ref))):
            tr = jnp.tanh(half_g[:, 2 * d * ct:(2 * d + 1) * ct])
            ti = jnp.tanh(half_g[:, (2 * d + 1) * ct:(2 * d + 2) * ct])
            a = jnp.exp2(tr * half_decay[d:d + 1, :] + half_decay[d:d + 1, :])
            y = 1.0 - a * a
            bterm = (y * lax.rsqrt(jnp.maximum(y, 1e-30))) * (ti * half_xc + half_xc)
            for sl in range(nslab):
                a_ref[sl, pl.ds(r0, tc), :] = a[:, sl * nl:(sl + 1) * nl]
                b_ref[sl, pl.ds(r0, tc), :] = bterm[:, sl * nl:(sl + 1) * nl]
        return 0

    lax.fori_loop(0, nchunks, phase1, 0)

    unroll = n // nslab

    def scan_group(tls, a_ref, b_ref, state):
        rows = [pl.ds(tl, RG_SEGMENTS, stride=pitch) for tl in tls]
        vals = [[(a_ref[sl, r, :], b_ref[sl, r, :]) for r in rows] for sl in range(nslab)]
        outs = []
        for sl in range(nslab):
            acum, h = state[sl]
            res = []
            for a, b in vals[sl]:
                acum = a * acum
                h = a * h + b
                res.append((acum, h))
            state[sl] = (acum, h)
            outs.append(res)
        for sl in range(nslab):
            for r, (acum, h) in zip(rows, outs[sl]):
                a_ref[sl, r, :] = acum
                b_ref[sl, r, :] = h

    ones = jnp.ones((RG_SEGMENTS, nl), F32)
    zero = jnp.zeros((RG_SEGMENTS, nl), F32)

    def phase2(it, carry):
        fwd = [carry[2 * sl] for sl in range(nslab)]
        rev = [carry[2 * sl + 1] for sl in range(nslab)]
        t0 = it * unroll
        scan_group([t0 + u for u in range(unroll)], af_ref, bf_ref, fwd)
        scan_group([seg_len - 1 - t0 - u for u in range(unroll)], ab_ref, bb_ref, rev)
        return tuple(x for sl in range(nslab) for x in (fwd[sl], rev[sl]))

    ends = lax.fori_loop(0, seg_len // unroll, phase2, tuple((ones, zero) for _ in range(2 * nslab)))

    seg_id = lax.broadcasted_iota(jnp.int32, (RG_SEGMENTS, nl), 0)
    for sl in range(nslab):
        for d, c_ref in enumerate((cf_ref, cr_ref)):
            acum, h = ends[2 * sl + d]
            carry = zero
            for k in range(1, RG_SEGMENTS):
                full = h + acum * carry
                if d == 0:
                    carry = jnp.where(seg_id == k, pltpu.roll(full, 1, 0), carry)
                else:
                    kk = RG_SEGMENTS - 1 - k
                    carry = jnp.where(seg_id == kk, pltpu.roll(full, RG_SEGMENTS - 1, 0), carry)
            c_ref[:, sl * nl:(sl + 1) * nl] = carry

    def phase3(c, _):
        t0 = pl.multiple_of(c * tc, tc)
        seg, r0 = seg_row(t0)
        cf = cf_ref[pl.ds(seg, 1), :]
        cr = cr_ref[pl.ds(seg, 1), :]
        for sl in range(nslab):
            lanes = slice(sl * nl, (sl + 1) * nl)
            hf = bf_ref[sl, pl.ds(r0, tc), :] + af_ref[sl, pl.ds(r0, tc), :] * cf[:, lanes]
            hr = bb_ref[sl, pl.ds(r0, tc), :] + ab_ref[sl, pl.ds(r0, tc), :] * cr[:, lanes]
            y_ref[0, pl.ds(t0, tc), lanes] = (hf + hr) * jax.nn.gelu(g_ref[0, pl.ds(t0, tc), lanes])
        return 0

    lax.fori_loop(0, nchunks, phase3, 0)


def _rglru(xrg, grg, cw, cb, w, bias, lam):
    b, s, c = xrg.shape
    ct = w.shape[1]
    seg_len = s // RG_SEGMENTS
    tc = _pick(seg_len, 256)
    seg_rows = RG_SEGMENTS * (seg_len + RG_SEG_PAD)
    seg_scratch = pltpu.VMEM((ct // V7X_LANES, seg_rows, V7X_LANES), F32)
    return pl.pallas_call(
        functools.partial(_rglru_body, tc=tc),
        grid=(b, c // ct),
        in_specs=[
            pl.BlockSpec((1, s, ct), lambda bi, j: (bi, 0, j)),
            pl.BlockSpec((1, s, ct), lambda bi, j: (bi, 0, j)),
            pl.BlockSpec((CONV_W, ct), lambda bi, j: (0, j)),
            pl.BlockSpec((1, ct), lambda bi, j: (0, j)),
            pl.BlockSpec((1, ct, 4 * ct), lambda bi, j: (j, 0, 0)),
            pl.BlockSpec((1, 1, 4 * ct), lambda bi, j: (j, 0, 0)),
            pl.BlockSpec((2, ct), lambda bi, j: (0, j)),
        ],
        out_specs=pl.BlockSpec((1, s, ct), lambda bi, j: (bi, 0, j)),
        out_shape=jax.ShapeDtypeStruct((b, s, c), F32),
        scratch_shapes=[pltpu.VMEM((s + 2 * V7X_SUBLANES, ct), F32),
                        seg_scratch, seg_scratch, seg_scratch, seg_scratch,
                        pltpu.VMEM((RG_SEGMENTS, ct), F32), pltpu.VMEM((RG_SEGMENTS, ct), F32)],
        compiler_params=pltpu.CompilerParams(
            dimension_semantics=("parallel", "parallel"),
            vmem_limit_bytes=V7X_VMEM_LIMIT_BYTES),
        name="rglru",
    )(xrg, grg, cw, cb, w, bias, lam)


LAZY_MAX_JUMP = 64.0
LAZY_MAX_SEED_KEYS = 256


def _attn_body(qt_ref, k_ref, vt_ref, o_ref, *, tk):
    tq = qt_ref.shape[3]
    s = k_ref.shape[2]
    qt = qt_ref[0, 0]
    nk = s // tk

    def scores(r0, rows):
        return jnp.dot(k_ref[0, 0, r0:r0 + rows, :], qt, preferred_element_type=F32)

    def pv(j, p):
        return jnp.dot(vt_ref[0, 0, :, j * tk:(j + 1) * tk], p.astype(BF16), preferred_element_type=F32)

    m = jnp.max(scores(0, min(LAZY_MAX_SEED_KEYS, tk)), axis=0, keepdims=True)
    l = jnp.zeros((1, tq), F32)
    acc = jnp.zeros((V_DIM, tq), F32)
    jump = jnp.zeros((1, tq), F32)
    for j in range(nk):
        st = scores(j * tk, tk)
        p = jnp.exp2(st - m)
        cmax = jnp.max(st, axis=0, keepdims=True)
        jump = jnp.maximum(jump, cmax - m)
        m_new = jnp.maximum(m, cmax)
        alpha = jnp.exp2(m - m_new)
        l = alpha * (l + jnp.sum(p, axis=0, keepdims=True))
        acc = alpha * (acc + pv(j, p))
        m = m_new
    o_ref[0] = (acc / l).T

    @pl.when(jnp.max(jump) > LAZY_MAX_JUMP)
    def _():
        m = jnp.full((1, tq), -jnp.inf, F32)
        l = jnp.zeros((1, tq), F32)
        acc = jnp.zeros((V_DIM, tq), F32)
        for j in range(nk):
            st = scores(j * tk, tk)
            m_new = jnp.maximum(m, jnp.max(st, axis=0, keepdims=True))
            p = jnp.exp2(st - m_new)
            alpha = jnp.exp2(m - m_new)
            l = alpha * l + jnp.sum(p, axis=0, keepdims=True)
            acc = alpha * acc + pv(j, p)
            m = m_new
        o_ref[0] = (acc / l).T


def _attention(qt, k, vt):
    b, nh, s, _ = k.shape
    tq = _pick(s, 2048 if s <= 2048 else 1024)
    tk = _pick(max(s // 2, V7X_LANES), 2048)
    return pl.pallas_call(
        functools.partial(_attn_body, tk=tk),
        grid=(b, nh, s // tq),
        in_specs=[
            pl.BlockSpec((1, 1, HEAD_PAD, tq), lambda bi, h, i: (bi, h, 0, i)),
            pl.BlockSpec((1, 1, s, HEAD_PAD), lambda bi, h, i: (bi, h, 0, 0)),
            pl.BlockSpec((1, 1, V_DIM, s), lambda bi, h, i: (bi, h, 0, 0)),
        ],
        out_specs=pl.BlockSpec((1, tq, V_DIM), lambda bi, h, i: (bi, i, h)),
        out_shape=jax.ShapeDtypeStruct((b, s, nh * V_DIM), F32),
        compiler_params=pltpu.CompilerParams(
            dimension_semantics=("parallel", "parallel", "parallel"),
            vmem_limit_bytes=V7X_VMEM_LIMIT_BYTES),
        name="attention",
    )(qt, k, vt)


def _mixer_out_body(x_ref, yr_ref, ya_ref, gr_ref, ga_ref, w_ref, o_ref):
    y = jnp.concatenate([_rms(yr_ref[...], gr_ref[...]).astype(BF16),
                         _rms(ya_ref[...], ga_ref[...]).astype(BF16)], axis=-1)
    o_ref[...] = x_ref[...] + jnp.dot(y, w_ref[...], preferred_element_type=F32)


def _mixer_out(x, yr, ya, gr, ga, w):
    t, d = x.shape
    tm = _pick(t, 512)
    return pl.pallas_call(
        _mixer_out_body,
        grid=(t // tm,),
        in_specs=[
            pl.BlockSpec((tm, d), lambda i: (i, 0)),
            pl.BlockSpec((tm, yr.shape[1]), lambda i: (i, 0)),
            pl.BlockSpec((tm, ya.shape[1]), lambda i: (i, 0)),
            _const_spec((1, yr.shape[1])),
            _const_spec((1, ya.shape[1])),
            _const_spec(w.shape),
        ],
        out_specs=pl.BlockSpec((tm, d), lambda i: (i, 0)),
        out_shape=jax.ShapeDtypeStruct((t, d), F32),
        compiler_params=pltpu.CompilerParams(
            dimension_semantics=("parallel",),
            vmem_limit_bytes=V7X_VMEM_LIMIT_BYTES),
        name="mixer_out",
    )(x, yr, ya, gr, ga, w)


def _rope_tables(seq):
    inv = 1.0 / (ROPE_THETA ** (jnp.arange(0, QK_ROPE, 2, dtype=F32) / QK_ROPE))
    ang = jnp.arange(seq, dtype=F32)[:, None] * inv[None, :]
    cos, sin = jnp.cos(ang), jnp.sin(ang)
    scale = math.log2(math.e) / math.sqrt(QK_NOPE + QK_ROPE)
    tqt = (scale * jnp.concatenate([jnp.ones((seq, QK_NOPE), F32), cos, cos, -sin, sin], axis=1)).T
    tkc = jnp.concatenate([cos, cos, cos, cos], axis=1)
    tks = jnp.concatenate([-sin, sin, -sin, sin], axis=1)
    return tqt, tkc, tks


def _layout_w_in(w_in):
    ko = 2 * RG_WIDTH + Q_LORA + KV_LORA
    k1 = w_in[:, ko:ko + ROPE_HALF]
    k2 = w_in[:, ko + ROPE_HALF:ko + QK_ROPE]
    return jnp.concatenate([w_in[:, :ko], k1, k2, k1, k2, k2, k1, k2, k1], axis=1).astype(BF16)


def _layout_w_q(w_q_b):
    w = w_q_b.reshape(Q_LORA, MLA_HEADS, QK_NOPE + QK_ROPE)
    x1 = w[:, :, QK_NOPE:QK_NOPE + ROPE_HALF]
    x2 = w[:, :, QK_NOPE + ROPE_HALF:]
    w = jnp.concatenate([w[:, :, :QK_NOPE], x1, x2, x2, x1], axis=-1)
    return w.reshape(Q_LORA, MLA_HEADS * HEAD_PAD).T.astype(BF16)


def _layout_w_kv(w_kv_b):
    w = w_kv_b.reshape(KV_LORA, MLA_HEADS, QK_NOPE + V_DIM)
    wk = w[:, :, :QK_NOPE].reshape(KV_LORA, -1)
    wvt = w[:, :, QK_NOPE:].reshape(KV_LORA, -1).T
    return wk.astype(BF16), wvt.astype(BF16)


def _layout_rg_gates(rg_w_a, rg_b_a, rg_w_x, rg_b_x, ct):
    g = ct // RG_BLOCK_W
    nt = RG_WIDTH // ct
    eye = jnp.eye(g, dtype=F32)

    def tiles(w):
        w = w.reshape(nt, g, RG_BLOCK_W, RG_BLOCK_W)
        return jnp.einsum("tgij,gk->tgikj", w, eye).reshape(nt, ct, ct)

    w = jnp.concatenate([tiles(rg_w_a[0]), tiles(rg_w_x[0]), tiles(rg_w_a[1]), tiles(rg_w_x[1])], axis=-1)
    bias = jnp.concatenate([rg_b_a[0].reshape(nt, 1, ct), rg_b_x[0].reshape(nt, 1, ct),
                            rg_b_a[1].reshape(nt, 1, ct), rg_b_x[1].reshape(nt, 1, ct)], axis=-1)
    return (0.5 * w).astype(BF16), 0.5 * bias


def _trunk(x, p):
    b, s, d = x.shape
    t = b * s
    row = lambda v: v.reshape(1, -1)
    x0 = x.reshape(t, d)
    x1 = _ffn(x0, row(p["ffn1_norm"]), p["ffn1_wg"], p["ffn1_wu"], p["ffn1_wd"], row(p["ffn1_norm"]),
              final_norm=False)

    tqt, tkc, tks = _rope_tables(s)
    xrg, grg, qt, k, vt = _mixer_in(x1.reshape(b, s, d), row(p["mix_norm"]), p["w_in"], row(p["q_a_norm"]),
                                    p["w_qt"], row(p["kv_a_norm"]), p["w_k"], p["w_vt"], tqt, tkc, tks)
    ct = 256 if s <= 4096 else 128
    gw, gb = p["rg_gates"][ct]
    y_rg = _rglru(xrg, grg, p["conv_w"], row(p["conv_b"]), gw, gb, p["rg_lambda"])
    y_att = _attention(qt, k, vt)
    x2 = _mixer_out(x1, y_rg.reshape(t, -1), y_att.reshape(t, -1), row(p["rg_out_norm"]),
                    row(p["attn_out_norm"]), p["w_out"])
    y = _ffn(x2, row(p["ffn2_norm"]), p["ffn2_wg"], p["ffn2_wu"], p["ffn2_wd"], row(p["final_norm"]),
             final_norm=True)
    return y.reshape(b, s, d)


def kernel(x_prompt, x_sample, ffn1_norm, ffn1_w_gate, ffn1_w_up, ffn1_w_down, mix_norm, w_in, conv_w, conv_b, rg_w_a, rg_b_a, rg_w_x, rg_b_x, rg_lambda, q_a_norm, w_q_b, kv_a_norm, w_kv_b, rg_out_norm, attn_out_norm, w_out, ffn2_norm, ffn2_w_gate, ffn2_w_up, ffn2_w_down, final_norm):
    depth = ffn1_norm.shape[0]

    def layer_params(l):
        w_k, w_vt = _layout_w_kv(w_kv_b[l])
        return {
            "ffn1_norm": ffn1_norm[l], "ffn1_wg": ffn1_w_gate[l].astype(BF16),
            "ffn1_wu": ffn1_w_up[l].astype(BF16), "ffn1_wd": ffn1_w_down[l].astype(BF16),
            "mix_norm": mix_norm[l], "w_in": _layout_w_in(w_in[l]),
            "conv_w": conv_w[l], "conv_b": conv_b[l],
            "rg_gates": {ct: _layout_rg_gates(rg_w_a[l], rg_b_a[l], rg_w_x[l], rg_b_x[l], ct)
                         for ct in (128, 256)},
            "rg_lambda": rg_lambda[l],
            "q_a_norm": q_a_norm[l], "w_qt": _layout_w_q(w_q_b[l]),
            "kv_a_norm": kv_a_norm[l], "w_k": w_k, "w_vt": w_vt,
            "rg_out_norm": rg_out_norm[l], "attn_out_norm": attn_out_norm[l],
            "w_out": w_out[l].astype(BF16),
            "ffn2_norm": ffn2_norm[l], "ffn2_wg": ffn2_w_gate[l].astype(BF16),
            "ffn2_wu": ffn2_w_up[l].astype(BF16), "ffn2_wd": ffn2_w_down[l].astype(BF16),
            "final_norm": final_norm,
        }

    assert depth == 1, "the final rmsnorm is fused into the last layer's second FFN"
    p = layer_params(0)
    return (_trunk(x_prompt, p), _trunk(x_sample, p))
```

```python
import functools
import math

import jax
import jax.numpy as jnp
from jax import lax
from jax.experimental import pallas as pl
from jax.experimental.pallas import tpu as pltpu

F32 = jnp.float32
BF16 = jnp.bfloat16

NORM_EPS = 1e-6
RG_WIDTH = 1024
RG_BLOCKS = 16
RG_BLOCK_W = RG_WIDTH // RG_BLOCKS
RG_C = 8.0
CONV_W = 4
CONV_PAD_L = 2
MLA_HEADS = 8
Q_LORA = 512
KV_LORA = 256
QK_NOPE = 128
QK_ROPE = 64
V_DIM = 128
ROPE_THETA = 10000.0
ROPE_HALF = QK_ROPE // 2
HEAD_PAD = 256

V7X_SUBLANES = 8
V7X_LANES = 128
V7X_VMEM_LIMIT_BYTES = 60000 * 1024


def _pick(dim, pref):
    t = min(dim, pref)
    while dim % t:
        t //= 2
    return t


def _rms(x, g):
    return x * lax.rsqrt(jnp.mean(x * x, axis=-1, keepdims=True) + NORM_EPS) * g


def _const_spec(shape):
    nd = len(shape)
    return pl.BlockSpec(shape, lambda *_: (0,) * nd, pipeline_mode=pl.Buffered(1))


def _ffn_body(x_ref, g_ref, wg_ref, wu_ref, wd_ref, pg_ref, o_ref, xn_ref, *, final_norm):
    f = pl.program_id(1)

    def half_swiglu(xn):
        gate = jnp.dot(xn, wg_ref[...], preferred_element_type=F32)
        up = jnp.dot(xn, wu_ref[...], preferred_element_type=F32)
        h = (gate * jax.nn.sigmoid(gate)) * (0.5 * up)
        return jnp.dot(h.astype(BF16), wd_ref[...], preferred_element_type=F32)

    @pl.when(f == 0)
    def _():
        x = x_ref[...]
        xn = _rms(x, g_ref[...]).astype(BF16)
        xn_ref[...] = xn
        o_ref[...] = x + half_swiglu(xn)

    @pl.when(f > 0)
    def _():
        o_ref[...] += half_swiglu(xn_ref[...])

    if final_norm:
        @pl.when(f == pl.num_programs(1) - 1)
        def _():
            o_ref[...] = _rms(o_ref[...], pg_ref[...])


def _ffn(x, g, wg, wu, wd, pg, *, final_norm):
    t, d = x.shape
    dff = wg.shape[1]
    tm = _pick(t, 1024)
    tf = _pick(dff, 512)
    return pl.pallas_call(
        functools.partial(_ffn_body, final_norm=final_norm),
        grid=(t // tm, dff // tf),
        in_specs=[
            pl.BlockSpec((tm, d), lambda i, f: (i, 0)),
            _const_spec((1, d)),
            pl.BlockSpec((d, tf), lambda i, f: (0, f)),
            pl.BlockSpec((d, tf), lambda i, f: (0, f)),
            pl.BlockSpec((tf, d), lambda i, f: (f, 0)),
            _const_spec((1, d)),
        ],
        out_specs=pl.BlockSpec((tm, d), lambda i, f: (i, 0)),
        out_shape=jax.ShapeDtypeStruct((t, d), F32),
        scratch_shapes=[pltpu.VMEM((tm, d), BF16)],
        compiler_params=pltpu.CompilerParams(
            dimension_semantics=("parallel", "arbitrary"),
            vmem_limit_bytes=V7X_VMEM_LIMIT_BYTES),
        name="ffn",
    )(x, g, wg, wu, wd, pg)


def _mixer_in_body(x_ref, g_ref, win_ref, qg_ref, wqt_ref, kvg_ref, wk_ref, wvt_ref, tqt_ref, tkc_ref, tks_ref,
                   xrg_ref, grg_ref, qt_ref, k_ref, vt_ref):
    hn = _rms(x_ref[0], g_ref[...]).astype(BF16)
    z = jnp.dot(hn, win_ref[...], preferred_element_type=F32)
    o = 0
    xrg_ref[0] = z[:, o:o + RG_WIDTH]
    o += RG_WIDTH
    grg_ref[0] = z[:, o:o + RG_WIDTH]
    o += RG_WIDTH
    qc = z[:, o:o + Q_LORA]
    o += Q_LORA
    kvc = z[:, o:o + KV_LORA]
    o += KV_LORA
    ka = z[:, o:o + V7X_LANES]
    kb = z[:, o + V7X_LANES:o + 2 * V7X_LANES]

    nt = (((1,), (1,)), ((), ()))
    qn = _rms(qc, qg_ref[...]).astype(BF16)
    kvn = _rms(kvc, kvg_ref[...]).astype(BF16)
    qt = lax.dot_general(wqt_ref[...], qn, nt, preferred_element_type=F32)
    vt = lax.dot_general(wvt_ref[...], kvn, nt, preferred_element_type=F32)
    kn = jnp.dot(kvn, wk_ref[...], preferred_element_type=F32)
    kr = (ka * tkc_ref[...] + kb * tks_ref[...]).astype(BF16)
    tqt = tqt_ref[...]
    for h in range(MLA_HEADS):
        qt_ref[0, h] = (qt[h * HEAD_PAD:(h + 1) * HEAD_PAD, :] * tqt).astype(BF16)
        k_ref[0, h, :, 0:QK_NOPE] = kn[:, h * QK_NOPE:(h + 1) * QK_NOPE].astype(BF16)
        k_ref[0, h, :, QK_NOPE:HEAD_PAD] = kr
        vt_ref[0, h] = vt[h * V_DIM:(h + 1) * V_DIM, :].astype(BF16)


def _mixer_in(x, g, win, qg, wqt, kvg, wk, wvt, tqt, tkc, tks):
    b, s, d = x.shape
    tm = _pick(s, 512)
    din = win.shape[1]
    nh = MLA_HEADS
    return pl.pallas_call(
        _mixer_in_body,
        grid=(b, s // tm),
        in_specs=[
            pl.BlockSpec((1, tm, d), lambda bi, i: (bi, i, 0)),
            _const_spec((1, d)),
            _const_spec((d, din)),
            _const_spec((1, Q_LORA)),
            _const_spec((nh * HEAD_PAD, Q_LORA)),
            _const_spec((1, KV_LORA)),
            _const_spec((KV_LORA, nh * QK_NOPE)),
            _const_spec((nh * V_DIM, KV_LORA)),
            pl.BlockSpec((HEAD_PAD, tm), lambda bi, i: (0, i)),
            pl.BlockSpec((tm, V7X_LANES), lambda bi, i: (i, 0)),
            pl.BlockSpec((tm, V7X_LANES), lambda bi, i: (i, 0)),
        ],
        out_specs=[
            pl.BlockSpec((1, tm, RG_WIDTH), lambda bi, i: (bi, i, 0)),
            pl.BlockSpec((1, tm, RG_WIDTH), lambda bi, i: (bi, i, 0)),
            pl.BlockSpec((1, nh, HEAD_PAD, tm), lambda bi, i: (bi, 0, 0, i)),
            pl.BlockSpec((1, nh, tm, HEAD_PAD), lambda bi, i: (bi, 0, i, 0)),
            pl.BlockSpec((1, nh, V_DIM, tm), lambda bi, i: (bi, 0, 0, i)),
        ],
        out_shape=[
            jax.ShapeDtypeStruct((b, s, RG_WIDTH), F32),
            jax.ShapeDtypeStruct((b, s, RG_WIDTH), F32),
            jax.ShapeDtypeStruct((b, nh, HEAD_PAD, s), BF16),
            jax.ShapeDtypeStruct((b, nh, s, HEAD_PAD), BF16),
            jax.ShapeDtypeStruct((b, nh, V_DIM, s), BF16),
        ],
        compiler_params=pltpu.CompilerParams(
            dimension_semantics=("parallel", "parallel"),
            vmem_limit_bytes=V7X_VMEM_LIMIT_BYTES),
        name="mixer_in",
    )(x, g, win, qg, wqt, kvg, wk, wvt, tqt, tkc, tks)


RG_SEGMENTS = V7X_SUBLANES
RG_SEG_PAD = V7X_SUBLANES


def _rglru_body(x_ref, g_ref, cw_ref, cb_ref, w_ref, bias_ref, lam_ref, y_ref,
                xpad_ref, af_ref, bf_ref, ab_ref, bb_ref, cf_ref, cr_ref, *, tc):
    s, ct = x_ref.shape[1], x_ref.shape[2]
    n = V7X_SUBLANES
    nl = V7X_LANES
    nslab = ct // nl
    seg_len = s // RG_SEGMENTS
    pitch = seg_len + RG_SEG_PAD
    nchunks = s // tc
    halo = n

    zeros = jnp.zeros((halo, ct), F32)
    xpad_ref[0:halo, :] = zeros
    xpad_ref[halo + s:halo + s + halo, :] = zeros
    xpad_ref[halo:halo + s, :] = x_ref[0]

    cw = cw_ref[...]
    cb = cb_ref[...]
    half_decay = (-0.5 * RG_C * math.log2(math.e)) * jax.nn.softplus(-lam_ref[...])
    bias = bias_ref[0]

    def seg_row(t0):
        seg = t0 // seg_len
        return seg, pl.multiple_of(seg * pitch + (t0 - seg * seg_len), n)

    def phase1(c, _):
        t0 = pl.multiple_of(c * tc, tc)
        blk = xpad_ref[pl.ds(t0, tc + 2 * halo), :]
        xc = cb
        for k in range(CONV_W):
            o = halo - CONV_PAD_L + k
            xc = xc + cw[k:k + 1, :] * blk[o:o + tc, :]
        half_g = jnp.dot(xc.astype(BF16), w_ref[0], preferred_element_type=F32) + bias
        half_xc = 0.5 * xc
        _, r0 = seg_row(t0)
        for d, (a_ref, b_ref) in enumerate(((af_ref, bf_ref), (ab_ref, bb_ref))):
            tr = jnp.tanh(half_g[:, 2 * d * ct:(2 * d + 1) * ct])
            ti = jnp.tanh(half_g[:, (2 * d + 1) * ct:(2 * d + 2) * ct])
            a = jnp.exp2(tr * half_decay[d:d + 1, :] + half_decay[d:d + 1, :])
            y = 1.0 - a * a
            bterm = (y * lax.rsqrt(jnp.maximum(y, 1e-30))) * (ti * half_xc + half_xc)
            for sl in range(nslab):
                a_ref[sl, pl.ds(r0, tc), :] = a[:, sl * nl:(sl + 1) * nl]
                b_ref[sl, pl.ds(r0, tc), :] = bterm[:, sl * nl:(sl + 1) * nl]
        return 0

    lax.fori_loop(0, nchunks, phase1, 0)

    unroll = n // nslab

    def scan_group(tls, a_ref, b_ref, state):
        rows = [pl.ds(tl, RG_SEGMENTS, stride=pitch) for tl in tls]
        vals = [[(a_ref[sl, r, :], b_ref[sl, r, :]) for r in rows] for sl in range(nslab)]
        outs = []
        for sl in range(nslab):
            acum, h = state[sl]
            res = []
            for a, b in vals[sl]:
                acum = a * acum
                h = a * h + b
                res.append((acum, h))
            state[sl] = (acum, h)
            outs.append(res)
        for sl in range(nslab):
            for r, (acum, h) in zip(rows, outs[sl]):
                a_ref[sl, r, :] = acum
                b_ref[sl, r, :] = h

    ones = jnp.ones((RG_SEGMENTS, nl), F32)
    zero = jnp.zeros((RG_SEGMENTS, nl), F32)

    def phase2(it, carry):
        fwd = [carry[2 * sl] for sl in range(nslab)]
        rev = [carry[2 * sl + 1] for sl in range(nslab)]
        t0 = it * unroll
        scan_group([t0 + u for u in range(unroll)], af_ref, bf_ref, fwd)
        scan_group([seg_len - 1 - t0 - u for u in range(unroll)], ab_ref, bb_ref, rev)
        return tuple(x for sl in range(nslab) for x in (fwd[sl], rev[sl]))

    ends = lax.fori_loop(0, seg_len // unroll, phase2, tuple((ones, zero) for _ in range(2 * nslab)))

    seg_id = lax.broadcasted_iota(jnp.int32, (RG_SEGMENTS, nl), 0)
    for sl in range(nslab):
        for d, c_ref in enumerate((cf_ref, cr_ref)):
            acum, h = ends[2 * sl + d]
            carry = zero
            for k in range(1, RG_SEGMENTS):
                full = h + acum * carry
                if d == 0:
                    carry = jnp.where(seg_id == k, pltpu.roll(full, 1, 0), carry)
                else:
                    kk = RG_SEGMENTS - 1 - k
                    carry = jnp.where(seg_id == kk, pltpu.roll(full, RG_SEGMENTS - 1, 0), carry)
            c_ref[:, sl * nl:(sl + 1) * nl] = carry

    def phase3(c, _):
        t0 = pl.multiple_of(c * tc, tc)
        seg, r0 = seg_row(t0)
        cf = cf_ref[pl.ds(seg, 1), :]
        cr = cr_ref[pl.ds(seg, 1), :]
        for sl in range(nslab):
            lanes = slice(sl * nl, (sl + 1) * nl)
            hf = bf_ref[sl, pl.ds(r0, tc), :] + af_ref[sl, pl.ds(r0, tc), :] * cf[:, lanes]
            hr = bb_ref[sl, pl.ds(r0, tc), :] + ab_ref[sl, pl.ds(r0, tc), :] * cr[:, lanes]
            y = (hf + hr) * jax.nn.gelu(g_ref[0, pl.ds(t0, tc), lanes])
            y_ref[0, pl.ds(t0, tc), lanes] = y.astype(y_ref.dtype)
        return 0

    lax.fori_loop(0, nchunks, phase3, 0)


def _rglru(xrg, grg, cw, cb, w, bias, lam):
    b, s, c = xrg.shape
    ct = w.shape[1]
    seg_len = s // RG_SEGMENTS
    tc = _pick(seg_len, 256)
    seg_rows = RG_SEGMENTS * (seg_len + RG_SEG_PAD)
    seg_scratch = pltpu.VMEM((ct // V7X_LANES, seg_rows, V7X_LANES), F32)
    return pl.pallas_call(
        functools.partial(_rglru_body, tc=tc),
        grid=(b, c // ct),
        in_specs=[
            pl.BlockSpec((1, s, ct), lambda bi, j: (bi, 0, j)),
            pl.BlockSpec((1, s, ct), lambda bi, j: (bi, 0, j)),
            pl.BlockSpec((CONV_W, ct), lambda bi, j: (0, j)),
            pl.BlockSpec((1, ct), lambda bi, j: (0, j)),
            pl.BlockSpec((1, ct, 4 * ct), lambda bi, j: (j, 0, 0)),
            pl.BlockSpec((1, 1, 4 * ct), lambda bi, j: (j, 0, 0)),
            pl.BlockSpec((2, ct), lambda bi, j: (0, j)),
        ],
        out_specs=pl.BlockSpec((1, s, ct), lambda bi, j: (bi, 0, j)),
        out_shape=jax.ShapeDtypeStruct((b, s, c), BF16),
        scratch_shapes=[pltpu.VMEM((s + 2 * V7X_SUBLANES, ct), F32),
                        seg_scratch, seg_scratch, seg_scratch, seg_scratch,
                        pltpu.VMEM((RG_SEGMENTS, ct), F32), pltpu.VMEM((RG_SEGMENTS, ct), F32)],
        compiler_params=pltpu.CompilerParams(
            dimension_semantics=("parallel", "parallel"),
            vmem_limit_bytes=V7X_VMEM_LIMIT_BYTES),
        name="rglru",
    )(xrg, grg, cw, cb, w, bias, lam)


LAZY_MAX_JUMP = 64.0
LAZY_MAX_SEED_KEYS = 256


def _attn_body(qt_ref, k_ref, vt_ref, o_ref, *, tk):
    tq = qt_ref.shape[3]
    s = k_ref.shape[2]
    qt = qt_ref[0, 0]
    nk = s // tk

    def scores(r0, rows):
        return jnp.dot(k_ref[0, 0, r0:r0 + rows, :], qt, preferred_element_type=F32)

    def pv(j, p):
        return jnp.dot(vt_ref[0, 0, :, j * tk:(j + 1) * tk], p.astype(BF16), preferred_element_type=F32)

    seed_rows = min(LAZY_MAX_SEED_KEYS, tk // 2)
    st_seed = scores(0, seed_rows)
    m = jnp.max(st_seed, axis=0, keepdims=True)
    l = jnp.zeros((1, tq), F32)
    acc = jnp.zeros((V_DIM, tq), F32)
    jump = jnp.zeros((1, tq), F32)
    for j in range(nk):
        if j == 0:
            st = jnp.concatenate([st_seed, scores(seed_rows, tk - seed_rows)], axis=0)
        else:
            st = scores(j * tk, tk)
        p = jnp.exp2(st - m)
        cmax = jnp.max(st, axis=0, keepdims=True)
        jump = jnp.maximum(jump, cmax - m)
        m_new = jnp.maximum(m, cmax)
        alpha = jnp.exp2(m - m_new)
        l = alpha * (l + jnp.sum(p, axis=0, keepdims=True))
        acc = alpha * (acc + pv(j, p))
        m = m_new
    o_ref[0] = (acc / l).T.astype(o_ref.dtype)

    @pl.when(jnp.max(jump) > LAZY_MAX_JUMP)
    def _():
        m = jnp.full((1, tq), -jnp.inf, F32)
        l = jnp.zeros((1, tq), F32)
        acc = jnp.zeros((V_DIM, tq), F32)
        for j in range(nk):
            st = scores(j * tk, tk)
            m_new = jnp.maximum(m, jnp.max(st, axis=0, keepdims=True))
            p = jnp.exp2(st - m_new)
            alpha = jnp.exp2(m - m_new)
            l = alpha * l + jnp.sum(p, axis=0, keepdims=True)
            acc = alpha * acc + pv(j, p)
            m = m_new
        o_ref[0] = (acc / l).T.astype(o_ref.dtype)


def _attention(qt, k, vt):
    b, nh, s, _ = k.shape
    tq = _pick(s, 2048 if s <= 2048 else 1024)
    tk = _pick(max(s // 2, V7X_LANES), 2048)
    return pl.pallas_call(
        functools.partial(_attn_body, tk=tk),
        grid=(b, nh, s // tq),
        in_specs=[
            pl.BlockSpec((1, 1, HEAD_PAD, tq), lambda bi, h, i: (bi, h, 0, i)),
            pl.BlockSpec((1, 1, s, HEAD_PAD), lambda bi, h, i: (bi, h, 0, 0)),
            pl.BlockSpec((1, 1, V_DIM, s), lambda bi, h, i: (bi, h, 0, 0)),
        ],
        out_specs=pl.BlockSpec((1, tq, V_DIM), lambda bi, h, i: (bi, i, h)),
        out_shape=jax.ShapeDtypeStruct((b, s, nh * V_DIM), BF16),
        compiler_params=pltpu.CompilerParams(
            dimension_semantics=("parallel", "parallel", "parallel"),
            vmem_limit_bytes=V7X_VMEM_LIMIT_BYTES),
        name="attention",
    )(qt, k, vt)


def _mixer_out_body(x_ref, yr_ref, ya_ref, gr_ref, ga_ref, w_ref, o_ref):
    y = jnp.concatenate([_rms(yr_ref[...].astype(F32), gr_ref[...]).astype(BF16),
                         _rms(ya_ref[...].astype(F32), ga_ref[...]).astype(BF16)], axis=-1)
    o_ref[...] = x_ref[...] + jnp.dot(y, w_ref[...], preferred_element_type=F32)


def _mixer_out(x, yr, ya, gr, ga, w):
    t, d = x.shape
    tm = _pick(t, 512)
    return pl.pallas_call(
        _mixer_out_body,
        grid=(t // tm,),
        in_specs=[
            pl.BlockSpec((tm, d), lambda i: (i, 0)),
            pl.BlockSpec((tm, yr.shape[1]), lambda i: (i, 0)),
            pl.BlockSpec((tm, ya.shape[1]), lambda i: (i, 0)),
            _const_spec((1, yr.shape[1])),
            _const_spec((1, ya.shape[1])),
            _const_spec(w.shape),
        ],
        out_specs=pl.BlockSpec((tm, d), lambda i: (i, 0)),
        out_shape=jax.ShapeDtypeStruct((t, d), F32),
        compiler_params=pltpu.CompilerParams(
            dimension_semantics=("parallel",),
            vmem_limit_bytes=V7X_VMEM_LIMIT_BYTES),
        name="mixer_out",
    )(x, yr, ya, gr, ga, w)


def _rope_tables(seq):
    inv = 1.0 / (ROPE_THETA ** (jnp.arange(0, QK_ROPE, 2, dtype=F32) / QK_ROPE))
    ang = jnp.arange(seq, dtype=F32)[:, None] * inv[None, :]
    cos, sin = jnp.cos(ang), jnp.sin(ang)
    scale = math.log2(math.e) / math.sqrt(QK_NOPE + QK_ROPE)
    tqt = (scale * jnp.concatenate([jnp.ones((seq, QK_NOPE), F32), cos, cos, -sin, sin], axis=1)).T
    tkc = jnp.concatenate([cos, cos, cos, cos], axis=1)
    tks = jnp.concatenate([-sin, sin, -sin, sin], axis=1)
    return tqt, tkc, tks


def _layout_w_in(w_in):
    ko = 2 * RG_WIDTH + Q_LORA + KV_LORA
    k1 = w_in[:, ko:ko + ROPE_HALF]
    k2 = w_in[:, ko + ROPE_HALF:ko + QK_ROPE]
    return jnp.concatenate([w_in[:, :ko], k1, k2, k1, k2, k2, k1, k2, k1], axis=1).astype(BF16)


def _layout_w_q(w_q_b):
    w = w_q_b.reshape(Q_LORA, MLA_HEADS, QK_NOPE + QK_ROPE)
    x1 = w[:, :, QK_NOPE:QK_NOPE + ROPE_HALF]
    x2 = w[:, :, QK_NOPE + ROPE_HALF:]
    w = jnp.concatenate([w[:, :, :QK_NOPE], x1, x2, x2, x1], axis=-1)
    return w.reshape(Q_LORA, MLA_HEADS * HEAD_PAD).T.astype(BF16)


def _layout_w_kv(w_kv_b):
    w = w_kv_b.reshape(KV_LORA, MLA_HEADS, QK_NOPE + V_DIM)
    wk = w[:, :, :QK_NOPE].reshape(KV_LORA, -1)
    wvt = w[:, :, QK_NOPE:].reshape(KV_LORA, -1).T
    return wk.astype(BF16), wvt.astype(BF16)


def _layout_rg_gates(rg_w_a, rg_b_a, rg_w_x, rg_b_x, ct):
    g = ct // RG_BLOCK_W
    nt = RG_WIDTH // ct
    eye = jnp.eye(g, dtype=F32)

    def tiles(w):
        w = w.reshape(nt, g, RG_BLOCK_W, RG_BLOCK_W)
        return jnp.einsum("tgij,gk->tgikj", w, eye).reshape(nt, ct, ct)

    w = jnp.concatenate([tiles(rg_w_a[0]), tiles(rg_w_x[0]), tiles(rg_w_a[1]), tiles(rg_w_x[1])], axis=-1)
    bias = jnp.concatenate([rg_b_a[0].reshape(nt, 1, ct), rg_b_x[0].reshape(nt, 1, ct),
                            rg_b_a[1].reshape(nt, 1, ct), rg_b_x[1].reshape(nt, 1, ct)], axis=-1)
    return (0.5 * w).astype(BF16), 0.5 * bias


def _trunk(x, p):
    b, s, d = x.shape
    t = b * s
    row = lambda v: v.reshape(1, -1)
    x0 = x.reshape(t, d)
    x1 = _ffn(x0, row(p["ffn1_norm"]), p["ffn1_wg"], p["ffn1_wu"], p["ffn1_wd"], row(p["ffn1_norm"]),
              final_norm=False)

    tqt, tkc, tks = _rope_tables(s)
    xrg, grg, qt, k, vt = _mixer_in(x1.reshape(b, s, d), row(p["mix_norm"]), p["w_in"], row(p["q_a_norm"]),
                                    p["w_qt"], row(p["kv_a_norm"]), p["w_k"], p["w_vt"], tqt, tkc, tks)
    ct = 256 if s <= 4096 else 128
    gw, gb = p["rg_gates"][ct]
    y_rg = _rglru(xrg, grg, p["conv_w"], row(p["conv_b"]), gw, gb, p["rg_lambda"])
    y_att = _attention(qt, k, vt)
    x2 = _mixer_out(x1, y_rg.reshape(t, -1), y_att.reshape(t, -1), row(p["rg_out_norm"]),
                    row(p["attn_out_norm"]), p["w_out"])
    y = _ffn(x2, row(p["ffn2_norm"]), p["ffn2_wg"], p["ffn2_wu"], p["ffn2_wd"], row(p["final_norm"]),
             final_norm=True)
    return y.reshape(b, s, d)


def kernel(x_prompt, x_sample, ffn1_norm, ffn1_w_gate, ffn1_w_up, ffn1_w_down, mix_norm, w_in, conv_w, conv_b, rg_w_a, rg_b_a, rg_w_x, rg_b_x, rg_lambda, q_a_norm, w_q_b, kv_a_norm, w_kv_b, rg_out_norm, attn_out_norm, w_out, ffn2_norm, ffn2_w_gate, ffn2_w_up, ffn2_w_down, final_norm):
    depth = ffn1_norm.shape[0]

    def layer_params(l):
        w_k, w_vt = _layout_w_kv(w_kv_b[l])
        return {
            "ffn1_norm": ffn1_norm[l], "ffn1_wg": ffn1_w_gate[l].astype(BF16),
            "ffn1_wu": ffn1_w_up[l].astype(BF16), "ffn1_wd": ffn1_w_down[l].astype(BF16),
            "mix_norm": mix_norm[l], "w_in": _layout_w_in(w_in[l]),
            "conv_w": conv_w[l], "conv_b": conv_b[l],
            "rg_gates": {ct: _layout_rg_gates(rg_w_a[l], rg_b_a[l], rg_w_x[l], rg_b_x[l], ct)
                         for ct in (128, 256)},
            "rg_lambda": rg_lambda[l],
            "q_a_norm": q_a_norm[l], "w_qt": _layout_w_q(w_q_b[l]),
            "kv_a_norm": kv_a_norm[l], "w_k": w_k, "w_vt": w_vt,
            "rg_out_norm": rg_out_norm[l], "attn_out_norm": attn_out_norm[l],
            "w_out": w_out[l].astype(BF16),
            "ffn2_norm": ffn2_norm[l], "ffn2_wg": ffn2_w_gate[l].astype(BF16),
            "ffn2_wu": ffn2_w_up[l].astype(BF16), "ffn2_wd": ffn2_w_down[l].astype(BF16),
            "final_norm": final_norm,
        }

    assert depth == 1, "the final rmsnorm is fused into the last layer's second FFN"
    p = layer_params(0)
    return (_trunk(x_prompt, p), _trunk(x_sample, p))
```

```python
import functools
import math

import jax
import jax.numpy as jnp
from jax import lax
from jax.experimental import pallas as pl
from jax.experimental.pallas import tpu as pltpu

F32 = jnp.float32
BF16 = jnp.bfloat16

NORM_EPS = 1e-6
RG_WIDTH = 1024
RG_BLOCKS = 16
RG_BLOCK_W = RG_WIDTH // RG_BLOCKS
RG_C = 8.0
CONV_W = 4
CONV_PAD_L = 2
MLA_HEADS = 8
Q_LORA = 512
KV_LORA = 256
QK_NOPE = 128
QK_ROPE = 64
V_DIM = 128
ROPE_THETA = 10000.0
ROPE_HALF = QK_ROPE // 2
HEAD_PAD = 256

V7X_SUBLANES = 8
V7X_LANES = 128
V7X_VMEM_LIMIT_BYTES = 60000 * 1024


def _pick(dim, pref):
    t = min(dim, pref)
    while dim % t:
        t //= 2
    return t


def _rms(x, g):
    return x * lax.rsqrt(jnp.mean(x * x, axis=-1, keepdims=True) + NORM_EPS) * g


def _const_spec(shape):
    nd = len(shape)
    return pl.BlockSpec(shape, lambda *_: (0,) * nd, pipeline_mode=pl.Buffered(1))


def _ffn_body(x_ref, g_ref, wg_ref, wu_ref, wd_ref, pg_ref, o_ref, xn_ref, *, final_norm):
    f = pl.program_id(1)

    def half_swiglu(xn):
        gate = jnp.dot(xn, wg_ref[...], preferred_element_type=F32)
        up = jnp.dot(xn, wu_ref[...], preferred_element_type=F32)
        h = (gate * jax.nn.sigmoid(gate)) * (0.5 * up)
        return jnp.dot(h.astype(BF16), wd_ref[...], preferred_element_type=F32)

    @pl.when(f == 0)
    def _():
        x = x_ref[...]
        xn = _rms(x, g_ref[...]).astype(BF16)
        xn_ref[...] = xn
        o_ref[...] = x + half_swiglu(xn)

    @pl.when(f > 0)
    def _():
        o_ref[...] += half_swiglu(xn_ref[...])

    if final_norm:
        @pl.when(f == pl.num_programs(1) - 1)
        def _():
            o_ref[...] = _rms(o_ref[...], pg_ref[...])


def _ffn(x, g, wg, wu, wd, pg, *, final_norm):
    t, d = x.shape
    dff = wg.shape[1]
    tm = _pick(t, 1024)
    tf = _pick(dff, 512)
    return pl.pallas_call(
        functools.partial(_ffn_body, final_norm=final_norm),
        grid=(t // tm, dff // tf),
        in_specs=[
            pl.BlockSpec((tm, d), lambda i, f: (i, 0)),
            _const_spec((1, d)),
            pl.BlockSpec((d, tf), lambda i, f: (0, f)),
            pl.BlockSpec((d, tf), lambda i, f: (0, f)),
            pl.BlockSpec((tf, d), lambda i, f: (f, 0)),
            _const_spec((1, d)),
        ],
        out_specs=pl.BlockSpec((tm, d), lambda i, f: (i, 0)),
        out_shape=jax.ShapeDtypeStruct((t, d), F32),
        scratch_shapes=[pltpu.VMEM((tm, d), BF16)],
        compiler_params=pltpu.CompilerParams(
            dimension_semantics=("parallel", "arbitrary"),
            vmem_limit_bytes=V7X_VMEM_LIMIT_BYTES),
        name="ffn",
    )(x, g, wg, wu, wd, pg)


def _mixer_in_body(x_ref, g_ref, win_ref, qg_ref, wqt_ref, kvg_ref, wk_ref, wvt_ref, tqt_ref, tkc_ref, tks_ref,
                   xrg_ref, grg_ref, qt_ref, k_ref, vt_ref):
    hn = _rms(x_ref[0], g_ref[...]).astype(BF16)
    z = jnp.dot(hn, win_ref[...], preferred_element_type=F32)
    o = 0
    xrg_ref[0] = z[:, o:o + RG_WIDTH]
    o += RG_WIDTH
    grg_ref[0] = z[:, o:o + RG_WIDTH]
    o += RG_WIDTH
    qc = z[:, o:o + Q_LORA]
    o += Q_LORA
    kvc = z[:, o:o + KV_LORA]
    o += KV_LORA
    ka = z[:, o:o + V7X_LANES]
    kb = z[:, o + V7X_LANES:o + 2 * V7X_LANES]

    nt = (((1,), (1,)), ((), ()))
    qn = _rms(qc, qg_ref[...]).astype(BF16)
    kvn = _rms(kvc, kvg_ref[...]).astype(BF16)
    qt = lax.dot_general(wqt_ref[...], qn, nt, preferred_element_type=F32)
    vt = lax.dot_general(wvt_ref[...], kvn, nt, preferred_element_type=F32)
    kn = jnp.dot(kvn, wk_ref[...], preferred_element_type=F32)
    kr = (ka * tkc_ref[...] + kb * tks_ref[...]).astype(BF16)
    tqt = tqt_ref[...]
    for h in range(MLA_HEADS):
        qt_ref[0, h] = (qt[h * HEAD_PAD:(h + 1) * HEAD_PAD, :] * tqt).astype(BF16)
        k_ref[0, h, :, 0:QK_NOPE] = kn[:, h * QK_NOPE:(h + 1) * QK_NOPE].astype(BF16)
        k_ref[0, h, :, QK_NOPE:HEAD_PAD] = kr
        vt_ref[0, h] = vt[h * V_DIM:(h + 1) * V_DIM, :].astype(BF16)


def _mixer_in(x, g, win, qg, wqt, kvg, wk, wvt, tqt, tkc, tks):
    b, s, d = x.shape
    tm = _pick(s, 512)
    din = win.shape[1]
    nh = MLA_HEADS
    return pl.pallas_call(
        _mixer_in_body,
        grid=(b, s // tm),
        in_specs=[
            pl.BlockSpec((1, tm, d), lambda bi, i: (bi, i, 0)),
            _const_spec((1, d)),
            _const_spec((d, din)),
            _const_spec((1, Q_LORA)),
            _const_spec((nh * HEAD_PAD, Q_LORA)),
            _const_spec((1, KV_LORA)),
            _const_spec((KV_LORA, nh * QK_NOPE)),
            _const_spec((nh * V_DIM, KV_LORA)),
            pl.BlockSpec((HEAD_PAD, tm), lambda bi, i: (0, i)),
            pl.BlockSpec((tm, V7X_LANES), lambda bi, i: (i, 0)),
            pl.BlockSpec((tm, V7X_LANES), lambda bi, i: (i, 0)),
        ],
        out_specs=[
            pl.BlockSpec((1, tm, RG_WIDTH), lambda bi, i: (bi, i, 0)),
            pl.BlockSpec((1, tm, RG_WIDTH), lambda bi, i: (bi, i, 0)),
            pl.BlockSpec((1, nh, HEAD_PAD, tm), lambda bi, i: (bi, 0, 0, i)),
            pl.BlockSpec((1, nh, tm, HEAD_PAD), lambda bi, i: (bi, 0, i, 0)),
            pl.BlockSpec((1, nh, V_DIM, tm), lambda bi, i: (bi, 0, 0, i)),
        ],
        out_shape=[
            jax.ShapeDtypeStruct((b, s, RG_WIDTH), F32),
            jax.ShapeDtypeStruct((b, s, RG_WIDTH), F32),
            jax.ShapeDtypeStruct((b, nh, HEAD_PAD, s), BF16),
            jax.ShapeDtypeStruct((b, nh, s, HEAD_PAD), BF16),
            jax.ShapeDtypeStruct((b, nh, V_DIM, s), BF16),
        ],
        compiler_params=pltpu.CompilerParams(
            dimension_semantics=("parallel", "parallel"),
            vmem_limit_bytes=V7X_VMEM_LIMIT_BYTES),
        name="mixer_in",
    )(x, g, win, qg, wqt, kvg, wk, wvt, tqt, tkc, tks)


RG_SEGMENTS = V7X_SUBLANES
RG_SEG_PAD = V7X_SUBLANES


def _rglru_body(x_ref, g_ref, cw_ref, cb_ref, w_ref, bias_ref, lam_ref, y_ref,
                xpad_ref, af_ref, bf_ref, ab_ref, bb_ref, cf_ref, cr_ref, *, tc):
    s, ct = x_ref.shape[1], x_ref.shape[2]
    n = V7X_SUBLANES
    nl = V7X_LANES
    nslab = ct // nl
    seg_len = s // RG_SEGMENTS
    pitch = seg_len + RG_SEG_PAD
    nchunks = s // tc
    halo = n

    zeros = jnp.zeros((halo, ct), F32)
    xpad_ref[0:halo, :] = zeros
    xpad_ref[halo + s:halo + s + halo, :] = zeros
    xpad_ref[halo:halo + s, :] = x_ref[0]

    cw = cw_ref[...]
    cb = cb_ref[...]
    half_decay = (-0.5 * RG_C * math.log2(math.e)) * jax.nn.softplus(-lam_ref[...])
    bias = bias_ref[0]

    def seg_row(t0):
        seg = t0 // seg_len
        return seg, pl.multiple_of(seg * pitch + (t0 - seg * seg_len), n)

    def phase1(c, _):
        t0 = pl.multiple_of(c * tc, tc)
        blk = xpad_ref[pl.ds(t0, tc + 2 * halo), :]
        xc = cb
        for k in range(CONV_W):
            o = halo - CONV_PAD_L + k
            xc = xc + cw[k:k + 1, :] * blk[o:o + tc, :]
        half_g = jnp.dot(xc.astype(BF16), w_ref[0], preferred_element_type=F32) + bias
        half_xc = 0.5 * xc
        _, r0 = seg_row(t0)
        for d, (a_ref, b_ref) in enumerate(((af_ref, bf_ref), (ab_ref, bb_ref))):
            tr = jnp.tanh(half_g[:, 2 * d * ct:(2 * d + 1) * ct])
            ti = jnp.tanh(half_g[:, (2 * d + 1) * ct:(2 * d + 2) * ct])
            a = jnp.exp2(tr * half_decay[d:d + 1, :] + half_decay[d:d + 1, :])
            y = 1.0 - a * a
            bterm = (y * lax.rsqrt(jnp.maximum(y, 1e-30))) * (ti * half_xc + half_xc)
            for sl in range(nslab):
                a_ref[sl, pl.ds(r0, tc), :] = a[:, sl * nl:(sl + 1) * nl]
                b_ref[sl, pl.ds(r0, tc), :] = bterm[:, sl * nl:(sl + 1) * nl]
        return 0

    lax.fori_loop(0, nchunks, phase1, 0)

    unroll = n // nslab

    def scan_group(tls, a_ref, b_ref, state):
        rows = [pl.ds(tl, RG_SEGMENTS, stride=pitch) for tl in tls]
        vals = [[(a_ref[sl, r, :], b_ref[sl, r, :]) for r in rows] for sl in range(nslab)]
        outs = []
        for sl in range(nslab):
            acum, h = state[sl]
            res = []
            for a, b in vals[sl]:
                acum = a * acum
                h = a * h + b
                res.append((acum, h))
            state[sl] = (acum, h)
            outs.append(res)
        for sl in range(nslab):
            for r, (acum, h) in zip(rows, outs[sl]):
                a_ref[sl, r, :] = acum
                b_ref[sl, r, :] = h

    ones = jnp.ones((RG_SEGMENTS, nl), F32)
    zero = jnp.zeros((RG_SEGMENTS, nl), F32)

    def phase2(it, carry):
        fwd = [carry[2 * sl] for sl in range(nslab)]
        rev = [carry[2 * sl + 1] for sl in range(nslab)]
        t0 = it * unroll
        scan_group([t0 + u for u in range(unroll)], af_ref, bf_ref, fwd)
        scan_group([seg_len - 1 - t0 - u for u in range(unroll)], ab_ref, bb_ref, rev)
        return tuple(x for sl in range(nslab) for x in (fwd[sl], rev[sl]))

    ends = lax.fori_loop(0, seg_len // unroll, phase2, tuple((ones, zero) for _ in range(2 * nslab)))

    seg_id = lax.broadcasted_iota(jnp.int32, (RG_SEGMENTS, nl), 0)
    for sl in range(nslab):
        for d, c_ref in enumerate((cf_ref, cr_ref)):
            acum, h = ends[2 * sl + d]
            carry = zero
            for k in range(1, RG_SEGMENTS):
                full = h + acum * carry
                if d == 0:
                    carry = jnp.where(seg_id == k, pltpu.roll(full, 1, 0), carry)
                else:
                    kk = RG_SEGMENTS - 1 - k
                    carry = jnp.where(seg_id == kk, pltpu.roll(full, RG_SEGMENTS - 1, 0), carry)
            c_ref[:, sl * nl:(sl + 1) * nl] = carry

    def phase3(c, _):
        t0 = pl.multiple_of(c * tc, tc)
        seg, r0 = seg_row(t0)
        cf = cf_ref[pl.ds(seg, 1), :]
        cr = cr_ref[pl.ds(seg, 1), :]
        for sl in range(nslab):
            lanes = slice(sl * nl, (sl + 1) * nl)
            hf = bf_ref[sl, pl.ds(r0, tc), :] + af_ref[sl, pl.ds(r0, tc), :] * cf[:, lanes]
            hr = bb_ref[sl, pl.ds(r0, tc), :] + ab_ref[sl, pl.ds(r0, tc), :] * cr[:, lanes]
            y_ref[0, pl.ds(t0, tc), lanes] = (hf + hr) * jax.nn.gelu(g_ref[0, pl.ds(t0, tc), lanes])
        return 0

    lax.fori_loop(0, nchunks, phase3, 0)


def _rglru(xrg, grg, cw, cb, w, bias, lam):
    b, s, c = xrg.shape
    ct = w.shape[1]
    seg_len = s // RG_SEGMENTS
    tc = _pick(seg_len, 256)
    seg_rows = RG_SEGMENTS * (seg_len + RG_SEG_PAD)
    seg_scratch = pltpu.VMEM((ct // V7X_LANES, seg_rows, V7X_LANES), F32)
    return pl.pallas_call(
        functools.partial(_rglru_body, tc=tc),
        grid=(b, c // ct),
        in_specs=[
            pl.BlockSpec((1, s, ct), lambda bi, j: (bi, 0, j)),
            pl.BlockSpec((1, s, ct), lambda bi, j: (bi, 0, j)),
            pl.BlockSpec((CONV_W, ct), lambda bi, j: (0, j)),
            pl.BlockSpec((1, ct), lambda bi, j: (0, j)),
            pl.BlockSpec((1, ct, 4 * ct), lambda bi, j: (j, 0, 0)),
            pl.BlockSpec((1, 1, 4 * ct), lambda bi, j: (j, 0, 0)),
            pl.BlockSpec((2, ct), lambda bi, j: (0, j)),
        ],
        out_specs=pl.BlockSpec((1, s, ct), lambda bi, j: (bi, 0, j)),
        out_shape=jax.ShapeDtypeStruct((b, s, c), F32),
        scratch_shapes=[pltpu.VMEM((s + 2 * V7X_SUBLANES, ct), F32),
                        seg_scratch, seg_scratch, seg_scratch, seg_scratch,
                        pltpu.VMEM((RG_SEGMENTS, ct), F32), pltpu.VMEM((RG_SEGMENTS, ct), F32)],
        compiler_params=pltpu.CompilerParams(
            dimension_semantics=("parallel", "parallel"),
            vmem_limit_bytes=V7X_VMEM_LIMIT_BYTES),
        name="rglru",
    )(xrg, grg, cw, cb, w, bias, lam)


LAZY_MAX_JUMP = 64.0
LAZY_MAX_SEED_KEYS = 64


def _attn_body(qt_ref, k_ref, vt_ref, o_ref, *, tk):
    tq = qt_ref.shape[3]
    s = k_ref.shape[2]
    qt = qt_ref[0, 0]
    nk = s // tk

    def scores(r0, rows):
        return jnp.dot(k_ref[0, 0, r0:r0 + rows, :], qt, preferred_element_type=F32)

    def pv(j, p):
        return jnp.dot(vt_ref[0, 0, :, j * tk:(j + 1) * tk], p.astype(BF16), preferred_element_type=F32)

    m = jnp.max(scores(0, min(LAZY_MAX_SEED_KEYS, tk)), axis=0, keepdims=True)
    l = jnp.zeros((1, tq), F32)
    acc = jnp.zeros((V_DIM, tq), F32)
    jump = jnp.zeros((1, tq), F32)
    for j in range(nk):
        st = scores(j * tk, tk)
        p = jnp.exp2(st - m)
        cmax = jnp.max(st, axis=0, keepdims=True)
        jump = jnp.maximum(jump, cmax - m)
        m_new = jnp.maximum(m, cmax)
        alpha = jnp.exp2(m - m_new)
        l = alpha * (l + jnp.sum(p, axis=0, keepdims=True))
        acc = alpha * (acc + pv(j, p))
        m = m_new
    o_ref[0] = (acc / l).T

    @pl.when(jnp.max(jump) > LAZY_MAX_JUMP)
    def _():
        m = jnp.full((1, tq), -jnp.inf, F32)
        l = jnp.zeros((1, tq), F32)
        acc = jnp.zeros((V_DIM, tq), F32)
        for j in range(nk):
            st = scores(j * tk, tk)
            m_new = jnp.maximum(m, jnp.max(st, axis=0, keepdims=True))
            p = jnp.exp2(st - m_new)
            alpha = jnp.exp2(m - m_new)
            l = alpha * l + jnp.sum(p, axis=0, keepdims=True)
            acc = alpha * acc + pv(j, p)
            m = m_new
        o_ref[0] = (acc / l).T


def _attention(qt, k, vt):
    b, nh, s, _ = k.shape
    tq = _pick(s, 2048 if s <= 2048 else 1024)
    tk = _pick(max(s // 2, V7X_LANES), 2048)
    return pl.pallas_call(
        functools.partial(_attn_body, tk=tk),
        grid=(b, nh, s // tq),
        in_specs=[
            pl.BlockSpec((1, 1, HEAD_PAD, tq), lambda bi, h, i: (bi, h, 0, i)),
            pl.BlockSpec((1, 1, s, HEAD_PAD), lambda bi, h, i: (bi, h, 0, 0)),
            pl.BlockSpec((1, 1, V_DIM, s), lambda bi, h, i: (bi, h, 0, 0)),
        ],
        out_specs=pl.BlockSpec((1, tq, V_DIM), lambda bi, h, i: (bi, i, h)),
        out_shape=jax.ShapeDtypeStruct((b, s, nh * V_DIM), F32),
        compiler_params=pltpu.CompilerParams(
            dimension_semantics=("parallel", "parallel", "parallel"),
            vmem_limit_bytes=V7X_VMEM_LIMIT_BYTES),
        name="attention",
    )(qt, k, vt)


def _mixer_out_body(x_ref, yr_ref, ya_ref, gr_ref, ga_ref, w_ref, o_ref):
    y = jnp.concatenate([_rms(yr_ref[...], gr_ref[...]).astype(BF16),
                         _rms(ya_ref[...], ga_ref[...]).astype(BF16)], axis=-1)
    o_ref[...] = x_ref[...] + jnp.dot(y, w_ref[...], preferred_element_type=F32)


def _mixer_out(x, yr, ya, gr, ga, w):
    t, d = x.shape
    tm = _pick(t, 512)
    return pl.pallas_call(
        _mixer_out_body,
        grid=(t // tm,),
        in_specs=[
            pl.BlockSpec((tm, d), lambda i: (i, 0)),
            pl.BlockSpec((tm, yr.shape[1]), lambda i: (i, 0)),
            pl.BlockSpec((tm, ya.shape[1]), lambda i: (i, 0)),
            _const_spec((1, yr.shape[1])),
            _const_spec((1, ya.shape[1])),
            _const_spec(w.shape),
        ],
        out_specs=pl.BlockSpec((tm, d), lambda i: (i, 0)),
        out_shape=jax.ShapeDtypeStruct((t, d), F32),
        compiler_params=pltpu.CompilerParams(
            dimension_semantics=("parallel",),
            vmem_limit_bytes=V7X_VMEM_LIMIT_BYTES),
        name="mixer_out",
    )(x, yr, ya, gr, ga, w)


def _rope_tables(seq):
    inv = 1.0 / (ROPE_THETA ** (jnp.arange(0, QK_ROPE, 2, dtype=F32) / QK_ROPE))
    ang = jnp.arange(seq, dtype=F32)[:, None] * inv[None, :]
    cos, sin = jnp.cos(ang), jnp.sin(ang)
    scale = math.log2(math.e) / math.sqrt(QK_NOPE + QK_ROPE)
    tqt = (scale * jnp.concatenate([jnp.ones((seq, QK_NOPE), F32), cos, cos, -sin, sin], axis=1)).T
    tkc = jnp.concatenate([cos, cos, cos, cos], axis=1)
    tks = jnp.concatenate([-sin, sin, -sin, sin], axis=1)
    return tqt, tkc, tks


def _layout_w_in(w_in):
    ko = 2 * RG_WIDTH + Q_LORA + KV_LORA
    k1 = w_in[:, ko:ko + ROPE_HALF]
    k2 = w_in[:, ko + ROPE_HALF:ko + QK_ROPE]
    return jnp.concatenate([w_in[:, :ko], k1, k2, k1, k2, k2, k1, k2, k1], axis=1).astype(BF16)


def _layout_w_q(w_q_b):
    w = w_q_b.reshape(Q_LORA, MLA_HEADS, QK_NOPE + QK_ROPE)
    x1 = w[:, :, QK_NOPE:QK_NOPE + ROPE_HALF]
    x2 = w[:, :, QK_NOPE + ROPE_HALF:]
    w = jnp.concatenate([w[:, :, :QK_NOPE], x1, x2, x2, x1], axis=-1)
    return w.reshape(Q_LORA, MLA_HEADS * HEAD_PAD).T.astype(BF16)


def _layout_w_kv(w_kv_b):
    w = w_kv_b.reshape(KV_LORA, MLA_HEADS, QK_NOPE + V_DIM)
    wk = w[:, :, :QK_NOPE].reshape(KV_LORA, -1)
    wvt = w[:, :, QK_NOPE:].reshape(KV_LORA, -1).T
    return wk.astype(BF16), wvt.astype(BF16)


def _layout_rg_gates(rg_w_a, rg_b_a, rg_w_x, rg_b_x, ct):
    g = ct // RG_BLOCK_W
    nt = RG_WIDTH // ct
    eye = jnp.eye(g, dtype=F32)

    def tiles(w):
        w = w.reshape(nt, g, RG_BLOCK_W, RG_BLOCK_W)
        return jnp.einsum("tgij,gk->tgikj", w, eye).reshape(nt, ct, ct)

    w = jnp.concatenate([tiles(rg_w_a[0]), tiles(rg_w_x[0]), tiles(rg_w_a[1]), tiles(rg_w_x[1])], axis=-1)
    bias = jnp.concatenate([rg_b_a[0].reshape(nt, 1, ct), rg_b_x[0].reshape(nt, 1, ct),
                            rg_b_a[1].reshape(nt, 1, ct), rg_b_x[1].reshape(nt, 1, ct)], axis=-1)
    return (0.5 * w).astype(BF16), 0.5 * bias


def _trunk(x, p):
    b, s, d = x.shape
    t = b * s
    row = lambda v: v.reshape(1, -1)
    x0 = x.reshape(t, d)
    x1 = _ffn(x0, row(p["ffn1_norm"]), p["ffn1_wg"], p["ffn1_wu"], p["ffn1_wd"], row(p["ffn1_norm"]),
              final_norm=False)

    tqt, tkc, tks = _rope_tables(s)
    xrg, grg, qt, k, vt = _mixer_in(x1.reshape(b, s, d), row(p["mix_norm"]), p["w_in"], row(p["q_a_norm"]),
                                    p["w_qt"], row(p["kv_a_norm"]), p["w_k"], p["w_vt"], tqt, tkc, tks)
    ct = 256 if s <= 4096 else 128
    gw, gb = p["rg_gates"][ct]
    y_rg = _rglru(xrg, grg, p["conv_w"], row(p["conv_b"]), gw, gb, p["rg_lambda"])
    y_att = _attention(qt, k, vt)
    x2 = _mixer_out(x1, y_rg.reshape(t, -1), y_att.reshape(t, -1), row(p["rg_out_norm"]),
                    row(p["attn_out_norm"]), p["w_out"])
    y = _ffn(x2, row(p["ffn2_norm"]), p["ffn2_wg"], p["ffn2_wu"], p["ffn2_wd"], row(p["final_norm"]),
             final_norm=True)
    return y.reshape(b, s, d)


def kernel(x_prompt, x_sample, ffn1_norm, ffn1_w_gate, ffn1_w_up, ffn1_w_down, mix_norm, w_in, conv_w, conv_b, rg_w_a, rg_b_a, rg_w_x, rg_b_x, rg_lambda, q_a_norm, w_q_b, kv_a_norm, w_kv_b, rg_out_norm, attn_out_norm, w_out, ffn2_norm, ffn2_w_gate, ffn2_w_up, ffn2_w_down, final_norm):
    depth = ffn1_norm.shape[0]

    def layer_params(l):
        w_k, w_vt = _layout_w_kv(w_kv_b[l])
        return {
            "ffn1_norm": ffn1_norm[l], "ffn1_wg": ffn1_w_gate[l].astype(BF16),
            "ffn1_wu": ffn1_w_up[l].astype(BF16), "ffn1_wd": ffn1_w_down[l].astype(BF16),
            "mix_norm": mix_norm[l], "w_in": _layout_w_in(w_in[l]),
            "conv_w": conv_w[l], "conv_b": conv_b[l],
            "rg_gates": {ct: _layout_rg_gates(rg_w_a[l], rg_b_a[l], rg_w_x[l], rg_b_x[l], ct)
                         for ct in (128, 256)},
            "rg_lambda": rg_lambda[l],
            "q_a_norm": q_a_norm[l], "w_qt": _layout_w_q(w_q_b[l]),
            "kv_a_norm": kv_a_norm[l], "w_k": w_k, "w_vt": w_vt,
            "rg_out_norm": rg_out_norm[l], "attn_out_norm": attn_out_norm[l],
            "w_out": w_out[l].astype(BF16),
            "ffn2_norm": ffn2_norm[l], "ffn2_wg": ffn2_w_gate[l].astype(BF16),
            "ffn2_wu": ffn2_w_up[l].astype(BF16), "ffn2_wd": ffn2_w_down[l].astype(BF16),
            "final_norm": final_norm,
        }

    assert depth == 1, "the final rmsnorm is fused into the last layer's second FFN"
    p = layer_params(0)
    return (_trunk(x_prompt, p), _trunk(x_sample, p))
```

```python
import functools
import math

import jax
import jax.numpy as jnp
from jax import lax
from jax.experimental import pallas as pl
from jax.experimental.pallas import tpu as pltpu

F32 = jnp.float32
BF16 = jnp.bfloat16

NORM_EPS = 1e-6
RG_WIDTH = 1024
RG_BLOCKS = 16
RG_BLOCK_W = RG_WIDTH // RG_BLOCKS
RG_C = 8.0
CONV_W = 4
CONV_PAD_L = 2
MLA_HEADS = 8
Q_LORA = 512
KV_LORA = 256
QK_NOPE = 128
QK_ROPE = 64
V_DIM = 128
ROPE_THETA = 10000.0
ROPE_HALF = QK_ROPE // 2
HEAD_PAD = 256

V7X_SUBLANES = 8
V7X_LANES = 128
V7X_VMEM_LIMIT_BYTES = 60000 * 1024


def _pick(dim, pref):
    t = min(dim, pref)
    while dim % t:
        t //= 2
    return t


def _rms(x, g):
    return x * lax.rsqrt(jnp.mean(x * x, axis=-1, keepdims=True) + NORM_EPS) * g


def _const_spec(shape):
    nd = len(shape)
    return pl.BlockSpec(shape, lambda *_: (0,) * nd, pipeline_mode=pl.Buffered(1))


def _ffn_body(x_ref, g_ref, wg_ref, wu_ref, wd_ref, pg_ref, o_ref, xn_ref, *, final_norm):
    f = pl.program_id(1)

    def half_swiglu(xn):
        gate = jnp.dot(xn, wg_ref[...], preferred_element_type=F32)
        up = jnp.dot(xn, wu_ref[...], preferred_element_type=F32)
        h = (gate * jax.nn.sigmoid(gate)) * (0.5 * up)
        return jnp.dot(h.astype(BF16), wd_ref[...], preferred_element_type=F32)

    @pl.when(f == 0)
    def _():
        x = x_ref[...]
        xn = _rms(x, g_ref[...]).astype(BF16)
        xn_ref[...] = xn
        o_ref[...] = x + half_swiglu(xn)

    last = pl.num_programs(1) - 1
    if not final_norm:
        @pl.when(f > 0)
        def _():
            o_ref[...] += half_swiglu(xn_ref[...])
    else:
        @pl.when((f > 0) & (f < last))
        def _():
            o_ref[...] += half_swiglu(xn_ref[...])

        @pl.when(f == last)
        def _():
            o_ref[...] += half_swiglu(xn_ref[...])
            o_ref[...] = _rms(o_ref[...], pg_ref[...])


def _ffn(x, g, wg, wu, wd, pg, *, final_norm):
    t, d = x.shape
    dff = wg.shape[1]
    tm = _pick(t, 1024)
    tf = _pick(dff, 512)
    return pl.pallas_call(
        functools.partial(_ffn_body, final_norm=final_norm),
        grid=(t // tm, dff // tf),
        in_specs=[
            pl.BlockSpec((tm, d), lambda i, f: (i, 0)),
            _const_spec((1, d)),
            pl.BlockSpec((d, tf), lambda i, f: (0, f)),
            pl.BlockSpec((d, tf), lambda i, f: (0, f)),
            pl.BlockSpec((tf, d), lambda i, f: (f, 0)),
            _const_spec((1, d)),
        ],
        out_specs=pl.BlockSpec((tm, d), lambda i, f: (i, 0)),
        out_shape=jax.ShapeDtypeStruct((t, d), F32),
        scratch_shapes=[pltpu.VMEM((tm, d), BF16)],
        compiler_params=pltpu.CompilerParams(
            dimension_semantics=("parallel", "arbitrary"),
            vmem_limit_bytes=V7X_VMEM_LIMIT_BYTES),
        name="ffn",
    )(x, g, wg, wu, wd, pg)


def _mixer_in_body(x_ref, g_ref, win_ref, qg_ref, wqt_ref, kvg_ref, wk_ref, wvt_ref, tqt_ref, tkc_ref, tks_ref,
                   xrg_ref, grg_ref, qt_ref, k_ref, vt_ref):
    hn = _rms(x_ref[0], g_ref[...]).astype(BF16)
    z = jnp.dot(hn, win_ref[...], preferred_element_type=F32)
    o = 0
    xrg_ref[0] = z[:, o:o + RG_WIDTH]
    o += RG_WIDTH
    grg_ref[0] = z[:, o:o + RG_WIDTH]
    o += RG_WIDTH
    qc = z[:, o:o + Q_LORA]
    o += Q_LORA
    kvc = z[:, o:o + KV_LORA]
    o += KV_LORA
    ka = z[:, o:o + V7X_LANES]
    kb = z[:, o + V7X_LANES:o + 2 * V7X_LANES]

    nt = (((1,), (1,)), ((), ()))
    qn = _rms(qc, qg_ref[...]).astype(BF16)
    kvn = _rms(kvc, kvg_ref[...]).astype(BF16)
    qt = lax.dot_general(wqt_ref[...], qn, nt, preferred_element_type=F32)
    vt = lax.dot_general(wvt_ref[...], kvn, nt, preferred_element_type=F32)
    kn = jnp.dot(kvn, wk_ref[...], preferred_element_type=F32)
    kr = (ka * tkc_ref[...] + kb * tks_ref[...]).astype(BF16)
    tqt = tqt_ref[...]
    for h in range(MLA_HEADS):
        qt_ref[0, h] = (qt[h * HEAD_PAD:(h + 1) * HEAD_PAD, :] * tqt).astype(BF16)
        k_ref[0, h, :, 0:QK_NOPE] = kn[:, h * QK_NOPE:(h + 1) * QK_NOPE].astype(BF16)
        k_ref[0, h, :, QK_NOPE:HEAD_PAD] = kr
        vt_ref[0, h] = vt[h * V_DIM:(h + 1) * V_DIM, :].astype(BF16)


def _mixer_in(x, g, win, qg, wqt, kvg, wk, wvt, tqt, tkc, tks):
    b, s, d = x.shape
    tm = _pick(s, 512)
    din = win.shape[1]
    nh = MLA_HEADS
    return pl.pallas_call(
        _mixer_in_body,
        grid=(b, s // tm),
        in_specs=[
            pl.BlockSpec((1, tm, d), lambda bi, i: (bi, i, 0)),
            _const_spec((1, d)),
            _const_spec((d, din)),
            _const_spec((1, Q_LORA)),
            _const_spec((nh * HEAD_PAD, Q_LORA)),
            _const_spec((1, KV_LORA)),
            _const_spec((KV_LORA, nh * QK_NOPE)),
            _const_spec((nh * V_DIM, KV_LORA)),
            pl.BlockSpec((HEAD_PAD, tm), lambda bi, i: (0, i)),
            pl.BlockSpec((tm, V7X_LANES), lambda bi, i: (i, 0)),
            pl.BlockSpec((tm, V7X_LANES), lambda bi, i: (i, 0)),
        ],
        out_specs=[
            pl.BlockSpec((1, tm, RG_WIDTH), lambda bi, i: (bi, i, 0)),
            pl.BlockSpec((1, tm, RG_WIDTH), lambda bi, i: (bi, i, 0)),
            pl.BlockSpec((1, nh, HEAD_PAD, tm), lambda bi, i: (bi, 0, 0, i)),
            pl.BlockSpec((1, nh, tm, HEAD_PAD), lambda bi, i: (bi, 0, i, 0)),
            pl.BlockSpec((1, nh, V_DIM, tm), lambda bi, i: (bi, 0, 0, i)),
        ],
        out_shape=[
            jax.ShapeDtypeStruct((b, s, RG_WIDTH), F32),
            jax.ShapeDtypeStruct((b, s, RG_WIDTH), F32),
            jax.ShapeDtypeStruct((b, nh, HEAD_PAD, s), BF16),
            jax.ShapeDtypeStruct((b, nh, s, HEAD_PAD), BF16),
            jax.ShapeDtypeStruct((b, nh, V_DIM, s), BF16),
        ],
        compiler_params=pltpu.CompilerParams(
            dimension_semantics=("parallel", "parallel"),
            vmem_limit_bytes=V7X_VMEM_LIMIT_BYTES),
        name="mixer_in",
    )(x, g, win, qg, wqt, kvg, wk, wvt, tqt, tkc, tks)


RG_SEGMENTS = V7X_SUBLANES
RG_SEG_PAD = V7X_SUBLANES


def _rglru_body(x_ref, g_ref, cw_ref, cb_ref, w_ref, bias_ref, lam_ref, y_ref,
                xpad_ref, af_ref, bf_ref, ab_ref, bb_ref, cf_ref, cr_ref, *, tc):
    s, ct = x_ref.shape[1], x_ref.shape[2]
    n = V7X_SUBLANES
    nl = V7X_LANES
    nslab = ct // nl
    seg_len = s // RG_SEGMENTS
    pitch = seg_len + RG_SEG_PAD
    nchunks = s // tc
    halo = n

    zeros = jnp.zeros((halo, ct), F32)
    xpad_ref[0:halo, :] = zeros
    xpad_ref[halo + s:halo + s + halo, :] = zeros
    xpad_ref[halo:halo + s, :] = x_ref[0]

    cw = cw_ref[...]
    cb = cb_ref[...]
    half_decay = (-0.5 * RG_C * math.log2(math.e)) * jax.nn.softplus(-lam_ref[...])
    bias = bias_ref[0]

    def seg_row(t0):
        seg = t0 // seg_len
        return seg, pl.multiple_of(seg * pitch + (t0 - seg * seg_len), n)

    def phase1(c, _):
        t0 = pl.multiple_of(c * tc, tc)
        blk = xpad_ref[pl.ds(t0, tc + 2 * halo), :]
        xc = cb
        for k in range(CONV_W):
            o = halo - CONV_PAD_L + k
            xc = xc + cw[k:k + 1, :] * blk[o:o + tc, :]
        half_g = jnp.dot(xc.astype(BF16), w_ref[0], preferred_element_type=F32) + bias
        half_xc = 0.5 * xc
        _, r0 = seg_row(t0)
        for d, (a_ref, b_ref) in enumerate(((af_ref, bf_ref), (ab_ref, bb_ref))):
            tr = jnp.tanh(half_g[:, 2 * d * ct:(2 * d + 1) * ct])
            ti = jnp.tanh(half_g[:, (2 * d + 1) * ct:(2 * d + 2) * ct])
            a = jnp.exp2(tr * half_decay[d:d + 1, :] + half_decay[d:d + 1, :])
            y = 1.0 - a * a
            bterm = (y * lax.rsqrt(jnp.maximum(y, 1e-30))) * (ti * half_xc + half_xc)
            for sl in range(nslab):
                a_ref[sl, pl.ds(r0, tc), :] = a[:, sl * nl:(sl + 1) * nl]
                b_ref[sl, pl.ds(r0, tc), :] = bterm[:, sl * nl:(sl + 1) * nl]
        return 0

    lax.fori_loop(0, nchunks, phase1, 0)

    unroll = n // nslab

    def scan_group(tls, a_ref, b_ref, state):
        rows = [pl.ds(tl, RG_SEGMENTS, stride=pitch) for tl in tls]
        vals = [[(a_ref[sl, r, :], b_ref[sl, r, :]) for r in rows] for sl in range(nslab)]
        outs = []
        for sl in range(nslab):
            acum, h = state[sl]
            res = []
            for a, b in vals[sl]:
                acum = a * acum
                h = a * h + b
                res.append((acum, h))
            state[sl] = (acum, h)
            outs.append(res)
        for sl in range(nslab):
            for r, (acum, h) in zip(rows, outs[sl]):
                a_ref[sl, r, :] = acum
                b_ref[sl, r, :] = h

    ones = jnp.ones((RG_SEGMENTS, nl), F32)
    zero = jnp.zeros((RG_SEGMENTS, nl), F32)

    def phase2(it, carry):
        fwd = [carry[2 * sl] for sl in range(nslab)]
        rev = [carry[2 * sl + 1] for sl in range(nslab)]
        t0 = it * unroll
        scan_group([t0 + u for u in range(unroll)], af_ref, bf_ref, fwd)
        scan_group([seg_len - 1 - t0 - u for u in range(unroll)], ab_ref, bb_ref, rev)
        return tuple(x for sl in range(nslab) for x in (fwd[sl], rev[sl]))

    ends = lax.fori_loop(0, seg_len // unroll, phase2, tuple((ones, zero) for _ in range(2 * nslab)))

    seg_id = lax.broadcasted_iota(jnp.int32, (RG_SEGMENTS, nl), 0)
    for sl in range(nslab):
        for d, c_ref in enumerate((cf_ref, cr_ref)):
            acum, h = ends[2 * sl + d]
            carry = zero
            for k in range(1, RG_SEGMENTS):
                full = h + acum * carry
                if d == 0:
                    carry = jnp.where(seg_id == k, pltpu.roll(full, 1, 0), carry)
                else:
                    kk = RG_SEGMENTS - 1 - k
                    carry = jnp.where(seg_id == kk, pltpu.roll(full, RG_SEGMENTS - 1, 0), carry)
            c_ref[:, sl * nl:(sl + 1) * nl] = carry

    def phase3(c, _):
        t0 = pl.multiple_of(c * tc, tc)
        seg, r0 = seg_row(t0)
        cf = cf_ref[pl.ds(seg, 1), :]
        cr = cr_ref[pl.ds(seg, 1), :]
        for sl in range(nslab):
            lanes = slice(sl * nl, (sl + 1) * nl)
            hf = bf_ref[sl, pl.ds(r0, tc), :] + af_ref[sl, pl.ds(r0, tc), :] * cf[:, lanes]
            hr = bb_ref[sl, pl.ds(r0, tc), :] + ab_ref[sl, pl.ds(r0, tc), :] * cr[:, lanes]
            y_ref[0, pl.ds(t0, tc), lanes] = (hf + hr) * jax.nn.gelu(g_ref[0, pl.ds(t0, tc), lanes])
        return 0

    lax.fori_loop(0, nchunks, phase3, 0)


def _rglru(xrg, grg, cw, cb, w, bias, lam):
    b, s, c = xrg.shape
    ct = w.shape[1]
    seg_len = s // RG_SEGMENTS
    tc = _pick(seg_len, 256)
    seg_rows = RG_SEGMENTS * (seg_len + RG_SEG_PAD)
    seg_scratch = pltpu.VMEM((ct // V7X_LANES, seg_rows, V7X_LANES), F32)
    return pl.pallas_call(
        functools.partial(_rglru_body, tc=tc),
        grid=(b, c // ct),
        in_specs=[
            pl.BlockSpec((1, s, ct), lambda bi, j: (bi, 0, j)),
            pl.BlockSpec((1, s, ct), lambda bi, j: (bi, 0, j)),
            pl.BlockSpec((CONV_W, ct), lambda bi, j: (0, j)),
            pl.BlockSpec((1, ct), lambda bi, j: (0, j)),
            pl.BlockSpec((1, ct, 4 * ct), lambda bi, j: (j, 0, 0)),
            pl.BlockSpec((1, 1, 4 * ct), lambda bi, j: (j, 0, 0)),
            pl.BlockSpec((2, ct), lambda bi, j: (0, j)),
        ],
        out_specs=pl.BlockSpec((1, s, ct), lambda bi, j: (bi, 0, j)),
        out_shape=jax.ShapeDtypeStruct((b, s, c), F32),
        scratch_shapes=[pltpu.VMEM((s + 2 * V7X_SUBLANES, ct), F32),
                        seg_scratch, seg_scratch, seg_scratch, seg_scratch,
                        pltpu.VMEM((RG_SEGMENTS, ct), F32), pltpu.VMEM((RG_SEGMENTS, ct), F32)],
        compiler_params=pltpu.CompilerParams(
            dimension_semantics=("parallel", "parallel"),
            vmem_limit_bytes=V7X_VMEM_LIMIT_BYTES),
        name="rglru",
    )(xrg, grg, cw, cb, w, bias, lam)


LAZY_MAX_JUMP = 64.0
LAZY_MAX_SEED_KEYS = 64


def _attn_body(qt_ref, k_ref, vt_ref, o_ref, *, tk):
    tq = qt_ref.shape[3]
    s = k_ref.shape[2]
    qt = qt_ref[0, 0]
    nk = s // tk

    def scores(r0, rows):
        return jnp.dot(k_ref[0, 0, r0:r0 + rows, :], qt, preferred_element_type=F32)

    def pv(j, p):
        return jnp.dot(vt_ref[0, 0, :, j * tk:(j + 1) * tk], p.astype(BF16), preferred_element_type=F32)

    m = jnp.max(scores(0, min(LAZY_MAX_SEED_KEYS, tk)), axis=0, keepdims=True)
    l = jnp.zeros((1, tq), F32)
    acc = jnp.zeros((V_DIM, tq), F32)
    jump = jnp.zeros((1, tq), F32)
    for j in range(nk):
        st = scores(j * tk, tk)
        p = jnp.exp2(st - m)
        cmax = jnp.max(st, axis=0, keepdims=True)
        jump = jnp.maximum(jump, cmax - m)
        m_new = jnp.maximum(m, cmax)
        alpha = jnp.exp2(m - m_new)
        l = alpha * (l + jnp.sum(p, axis=0, keepdims=True))
        acc = alpha * (acc + pv(j, p))
        m = m_new
    o_ref[0] = (acc / l).T

    @pl.when(jnp.max(jump) > LAZY_MAX_JUMP)
    def _():
        m = jnp.full((1, tq), -jnp.inf, F32)
        l = jnp.zeros((1, tq), F32)
        acc = jnp.zeros((V_DIM, tq), F32)
        for j in range(nk):
            st = scores(j * tk, tk)
            m_new = jnp.maximum(m, jnp.max(st, axis=0, keepdims=True))
            p = jnp.exp2(st - m_new)
            alpha = jnp.exp2(m - m_new)
            l = alpha * l + jnp.sum(p, axis=0, keepdims=True)
            acc = alpha * acc + pv(j, p)
            m = m_new
        o_ref[0] = (acc / l).T


def _attention(qt, k, vt):
    b, nh, s, _ = k.shape
    tq = _pick(s, 2048 if s <= 2048 else 1024)
    tk = _pick(max(s // 2, V7X_LANES), 2048)
    return pl.pallas_call(
        functools.partial(_attn_body, tk=tk),
        grid=(b, nh, s // tq),
        in_specs=[
            pl.BlockSpec((1, 1, HEAD_PAD, tq), lambda bi, h, i: (bi, h, 0, i)),
            pl.BlockSpec((1, 1, s, HEAD_PAD), lambda bi, h, i: (bi, h, 0, 0)),
            pl.BlockSpec((1, 1, V_DIM, s), lambda bi, h, i: (bi, h, 0, 0)),
        ],
        out_specs=pl.BlockSpec((1, tq, V_DIM), lambda bi, h, i: (bi, i, h)),
        out_shape=jax.ShapeDtypeStruct((b, s, nh * V_DIM), F32),
        compiler_params=pltpu.CompilerParams(
            dimension_semantics=("parallel", "parallel", "parallel"),
            vmem_limit_bytes=V7X_VMEM_LIMIT_BYTES),
        name="attention",
    )(qt, k, vt)


def _mixer_out_body(x_ref, yr_ref, ya_ref, gr_ref, ga_ref, w_ref, o_ref):
    y = jnp.concatenate([_rms(yr_ref[...], gr_ref[...]).astype(BF16),
                         _rms(ya_ref[...], ga_ref[...]).astype(BF16)], axis=-1)
    o_ref[...] = x_ref[...] + jnp.dot(y, w_ref[...], preferred_element_type=F32)


def _mixer_out(x, yr, ya, gr, ga, w):
    t, d = x.shape
    tm = _pick(t, 512)
    return pl.pallas_call(
        _mixer_out_body,
        grid=(t // tm,),
        in_specs=[
            pl.BlockSpec((tm, d), lambda i: (i, 0)),
            pl.BlockSpec((tm, yr.shape[1]), lambda i: (i, 0)),
            pl.BlockSpec((tm, ya.shape[1]), lambda i: (i, 0)),
            _const_spec((1, yr.shape[1])),
            _const_spec((1, ya.shape[1])),
            _const_spec(w.shape),
        ],
        out_specs=pl.BlockSpec((tm, d), lambda i: (i, 0)),
        out_shape=jax.ShapeDtypeStruct((t, d), F32),
        compiler_params=pltpu.CompilerParams(
            dimension_semantics=("parallel",),
            vmem_limit_bytes=V7X_VMEM_LIMIT_BYTES),
        name="mixer_out",
    )(x, yr, ya, gr, ga, w)


def _rope_tables(seq):
    inv = 1.0 / (ROPE_THETA ** (jnp.arange(0, QK_ROPE, 2, dtype=F32) / QK_ROPE))
    ang = jnp.arange(seq, dtype=F32)[:, None] * inv[None, :]
    cos, sin = jnp.cos(ang), jnp.sin(ang)
    scale = math.log2(math.e) / math.sqrt(QK_NOPE + QK_ROPE)
    tqt = (scale * jnp.concatenate([jnp.ones((seq, QK_NOPE), F32), cos, cos, -sin, sin], axis=1)).T
    tkc = jnp.concatenate([cos, cos, cos, cos], axis=1)
    tks = jnp.concatenate([-sin, sin, -sin, sin], axis=1)
    return tqt, tkc, tks


def _layout_w_in(w_in):
    ko = 2 * RG_WIDTH + Q_LORA + KV_LORA
    k1 = w_in[:, ko:ko + ROPE_HALF]
    k2 = w_in[:, ko + ROPE_HALF:ko + QK_ROPE]
    return jnp.concatenate([w_in[:, :ko], k1, k2, k1, k2, k2, k1, k2, k1], axis=1).astype(BF16)


def _layout_w_q(w_q_b):
    w = w_q_b.reshape(Q_LORA, MLA_HEADS, QK_NOPE + QK_ROPE)
    x1 = w[:, :, QK_NOPE:QK_NOPE + ROPE_HALF]
    x2 = w[:, :, QK_NOPE + ROPE_HALF:]
    w = jnp.concatenate([w[:, :, :QK_NOPE], x1, x2, x2, x1], axis=-1)
    return w.reshape(Q_LORA, MLA_HEADS * HEAD_PAD).T.astype(BF16)


def _layout_w_kv(w_kv_b):
    w = w_kv_b.reshape(KV_LORA, MLA_HEADS, QK_NOPE + V_DIM)
    wk = w[:, :, :QK_NOPE].reshape(KV_LORA, -1)
    wvt = w[:, :, QK_NOPE:].reshape(KV_LORA, -1).T
    return wk.astype(BF16), wvt.astype(BF16)


def _layout_rg_gates(rg_w_a, rg_b_a, rg_w_x, rg_b_x, ct):
    g = ct // RG_BLOCK_W
    nt = RG_WIDTH // ct
    eye = jnp.eye(g, dtype=F32)

    def tiles(w):
        w = w.reshape(nt, g, RG_BLOCK_W, RG_BLOCK_W)
        return jnp.einsum("tgij,gk->tgikj", w, eye).reshape(nt, ct, ct)

    w = jnp.concatenate([tiles(rg_w_a[0]), tiles(rg_w_x[0]), tiles(rg_w_a[1]), tiles(rg_w_x[1])], axis=-1)
    bias = jnp.concatenate([rg_b_a[0].reshape(nt, 1, ct), rg_b_x[0].reshape(nt, 1, ct),
                            rg_b_a[1].reshape(nt, 1, ct), rg_b_x[1].reshape(nt, 1, ct)], axis=-1)
    return (0.5 * w).astype(BF16), 0.5 * bias


def _trunk(x, p):
    b, s, d = x.shape
    t = b * s
    row = lambda v: v.reshape(1, -1)
    x0 = x.reshape(t, d)
    x1 = _ffn(x0, row(p["ffn1_norm"]), p["ffn1_wg"], p["ffn1_wu"], p["ffn1_wd"], row(p["ffn1_norm"]),
              final_norm=False)

    tqt, tkc, tks = _rope_tables(s)
    xrg, grg, qt, k, vt = _mixer_in(x1.reshape(b, s, d), row(p["mix_norm"]), p["w_in"], row(p["q_a_norm"]),
                                    p["w_qt"], row(p["kv_a_norm"]), p["w_k"], p["w_vt"], tqt, tkc, tks)
    ct = 256 if s <= 4096 else 128
    gw, gb = p["rg_gates"][ct]
    y_rg = _rglru(xrg, grg, p["conv_w"], row(p["conv_b"]), gw, gb, p["rg_lambda"])
    y_att = _attention(qt, k, vt)
    x2 = _mixer_out(x1, y_rg.reshape(t, -1), y_att.reshape(t, -1), row(p["rg_out_norm"]),
                    row(p["attn_out_norm"]), p["w_out"])
    y = _ffn(x2, row(p["ffn2_norm"]), p["ffn2_wg"], p["ffn2_wu"], p["ffn2_wd"], row(p["final_norm"]),
             final_norm=True)
    return y.reshape(b, s, d)


def kernel(x_prompt, x_sample, ffn1_norm, ffn1_w_gate, ffn1_w_up, ffn1_w_down, mix_norm, w_in, conv_w, conv_b, rg_w_a, rg_b_a, rg_w_x, rg_b_x, rg_lambda, q_a_norm, w_q_b, kv_a_norm, w_kv_b, rg_out_norm, attn_out_norm, w_out, ffn2_norm, ffn2_w_gate, ffn2_w_up, ffn2_w_down, final_norm):
    depth = ffn1_norm.shape[0]

    def layer_params(l):
        w_k, w_vt = _layout_w_kv(w_kv_b[l])
        return {
            "ffn1_norm": ffn1_norm[l], "ffn1_wg": ffn1_w_gate[l].astype(BF16),
            "ffn1_wu": ffn1_w_up[l].astype(BF16), "ffn1_wd": ffn1_w_down[l].astype(BF16),
            "mix_norm": mix_norm[l], "w_in": _layout_w_in(w_in[l]),
            "conv_w": conv_w[l], "conv_b": conv_b[l],
            "rg_gates": {ct: _layout_rg_gates(rg_w_a[l], rg_b_a[l], rg_w_x[l], rg_b_x[l], ct)
                         for ct in (128, 256)},
            "rg_lambda": rg_lambda[l],
            "q_a_norm": q_a_norm[l], "w_qt": _layout_w_q(w_q_b[l]),
            "kv_a_norm": kv_a_norm[l], "w_k": w_k, "w_vt": w_vt,
            "rg_out_norm": rg_out_norm[l], "attn_out_norm": attn_out_norm[l],
            "w_out": w_out[l].astype(BF16),
            "ffn2_norm": ffn2_norm[l], "ffn2_wg": ffn2_w_gate[l].astype(BF16),
            "ffn2_wu": ffn2_w_up[l].astype(BF16), "ffn2_wd": ffn2_w_down[l].astype(BF16),
            "final_norm": final_norm,
        }

    assert depth == 1, "the final rmsnorm is fused into the last layer's second FFN"
    p = layer_params(0)
    return (_trunk(x_prompt, p), _trunk(x_sample, p))
```
